```python
import math
import jax, jax.numpy as jnp
from jax import lax
import numpy as np

D_MODEL = 1024
BATCH = 4
SEQ = 4096
DEPTH = 4
DEC_BATCH = 128
DEC_SEQ = 8
PAST_LEN = 8192
PAGE_SIZE = 128

N_MIXERS = 3
N_GMLP_LAYERS = (DEPTH + 2) // 3
N_CONV_LAYERS = (DEPTH + 1) // 3
N_MLA_LAYERS = DEPTH // 3

ALPHA = (2.0 * DEPTH) ** 0.25
BETA = (8.0 * DEPTH) ** -0.25
LN_EPS = 1e-5
RMS_EPS = 1e-6

GMLP_DIM = 2 * D_MODEL
GMLP_GROUPS = 8
GMLP_CHUNK = 128

CONV_WIDTH = 31

MLA_HEADS = 8
MLA_Q_RANK = 384
MLA_KV_RANK = 256
MLA_NOPE = 128
MLA_ROPE = 64
MLA_V = 128
ROPE_THETA = 10000.0
Q_BLOCK = 128
MLA_SCALE = (MLA_NOPE + MLA_ROPE) ** -0.5

FFN_DIM = ((8 * D_MODEL + 3 * 256 - 1) // (3 * 256)) * 256

kernel_name = 'hybrid_gmlp_conformer_mla_adaln_deepnorm_step'


def layer_norm(x, g, b):
    xf = x.astype(jnp.float32)
    mu = jnp.mean(xf, -1, keepdims=True)
    var = jnp.mean(jnp.square(xf - mu), -1, keepdims=True)
    return ((xf - mu) * lax.rsqrt(var + LN_EPS)).astype(x.dtype) * g + b


def rms_norm(x, g):
    xf = x.astype(jnp.float32)
    return (xf * lax.rsqrt(jnp.mean(jnp.square(xf), -1, keepdims=True) + RMS_EPS)).astype(x.dtype) * g


def rope(x, pos):
    half = x.shape[-1] // 2
    inv = ROPE_THETA ** (-jnp.arange(half, dtype=jnp.float32) / half)
    ang = pos.astype(jnp.float32)[:, None] * inv[None, :]
    cos = jnp.cos(ang)[None, :, None, :].astype(x.dtype)
    sin = jnp.sin(ang)[None, :, None, :].astype(x.dtype)
    x1, x2 = x[..., :half], x[..., half:]
    return jnp.concatenate([x1 * cos - x2 * sin, x2 * cos + x1 * sin], -1)


def adaln(c, w, b):
    m = (jax.nn.silu(c) @ w + b).reshape(c.shape[0], 6, 1, D_MODEL)
    return m[:, 0], m[:, 1], m[:, 2], m[:, 3], m[:, 4], m[:, 5]


def swiglu_ffn(h, wg, wu, wd):
    return (jax.nn.silu(h @ wg) * (h @ wu)) @ wd


def gmlp_mixer(h, w_in, ln_g, ln_b, w_s, b_s, w_out):
    bsz, t, _ = h.shape
    z = jax.nn.gelu(h @ w_in, approximate=False)
    u, v = z[..., :GMLP_DIM], z[..., GMLP_DIM:]
    v = layer_norm(v, ln_g, ln_b)
    l = min(t, GMLP_CHUNK)
    n = t // l
    cg = GMLP_DIM // GMLP_GROUPS
    causal = jnp.tril(jnp.ones((l, l), dtype=bool))
    w = jnp.where(causal[None], w_s[:, :l, :l], 0)
    vc = v.reshape(bsz, n, l, GMLP_GROUPS, cg)
    mixed = jnp.einsum('gts,bnsgc->bntgc', w, vc) + b_s[:, :l].T[None, None, :, :, None]
    y = (u * mixed.reshape(bsz, t, GMLP_DIM)) @ w_out
    return y, v


def conv_module(h, ctx, w_pw1, b_pw1, w_dw, b_dw, ln_g, ln_b, w_pw2, b_pw2):
    a = h @ w_pw1 + b_pw1
    a = a[..., :D_MODEL] * jax.nn.sigmoid(a[..., D_MODEL:])
    full = jnp.concatenate([ctx, a], 1)
    y = lax.conv_general_dilated(full, w_dw[:, None, :], window_strides=(1,), padding='VALID',
                                 dimension_numbers=('NWC', 'WIO', 'NWC'),
                                 feature_group_count=D_MODEL) + b_dw
    y = jax.nn.silu(layer_norm(y, ln_g, ln_b))
    return y @ w_pw2 + b_pw2, full[:, -(CONV_WIDTH - 1):]


def mla_project(h, pos, w_down, g_q, g_kv, w_uq, w_uk):
    bsz, t, _ = h.shape
    d = h @ w_down
    q_lat = rms_norm(d[..., :MLA_Q_RANK], g_q)
    ckv = rms_norm(d[..., MLA_Q_RANK:MLA_Q_RANK + MLA_KV_RANK], g_kv)
    kr = rope(d[..., MLA_Q_RANK + MLA_KV_RANK:][:, :, None, :], pos)[:, :, 0]
    q = (q_lat @ w_uq).reshape(bsz, t, MLA_HEADS, MLA_NOPE + MLA_ROPE)
    q_abs = jnp.einsum('bthn,chn->bthc', q[..., :MLA_NOPE], w_uk)
    q_rope = rope(q[..., MLA_NOPE:], pos)
    return q_abs, q_rope, ckv, kr


def mla_out(ctx, w_uv, w_o):
    bsz, t = ctx.shape[:2]
    o = jnp.einsum('bthc,chv->bthv', ctx, w_uv).reshape(bsz, t, MLA_HEADS * MLA_V)
    return o @ w_o


def mla_prompt_attend(q_abs, q_rope, ckv, kr):
    bsz, t = q_abs.shape[:2]
    nb = t // Q_BLOCK
    qa = q_abs.reshape(bsz, nb, Q_BLOCK, MLA_HEADS, MLA_KV_RANK).swapaxes(0, 1)
    qr = q_rope.reshape(bsz, nb, Q_BLOCK, MLA_HEADS, MLA_ROPE).swapaxes(0, 1)
    key_pos = jnp.arange(t)

    def block(args):
        qa_b, qr_b, start = args
        s = (jnp.einsum('bqhc,bkc->bhqk', qa_b, ckv)
             + jnp.einsum('bqhr,bkr->bhqk', qr_b, kr)).astype(jnp.float32) * MLA_SCALE
        q_pos = start + jnp.arange(Q_BLOCK)
        s = jnp.where(key_pos[None, :] <= q_pos[:, None], s, -jnp.inf)
        p = jax.nn.softmax(s, -1).astype(ckv.dtype)
        return jnp.einsum('bhqk,bkc->bqhc', p, ckv)

    ctx = lax.map(block, (qa, qr, jnp.arange(nb) * Q_BLOCK))
    return ctx.swapaxes(0, 1).reshape(bsz, t, MLA_HEADS, MLA_KV_RANK)


def mla_sample_attend(q_abs, q_rope, ckv_new, kr_new, ckv_past, kr_past):
    t = q_abs.shape[1]
    p_len = ckv_past.shape[1]
    s_past = (jnp.einsum('bqhc,bkc->bhqk', q_abs, ckv_past)
              + jnp.einsum('bqhr,bkr->bhqk', q_rope, kr_past)).astype(jnp.float32) * MLA_SCALE
    s_new = (jnp.einsum('bqhc,bkc->bhqk', q_abs, ckv_new)
             + jnp.einsum('bqhr,bkr->bhqk', q_rope, kr_new)).astype(jnp.float32) * MLA_SCALE
    s_new = jnp.where(jnp.tril(jnp.ones((t, t), dtype=bool)), s_new, -jnp.inf)
    p = jax.nn.softmax(jnp.concatenate([s_past, s_new], -1), -1).astype(ckv_new.dtype)
    return (jnp.einsum('bhqk,bkc->bqhc', p[..., :p_len], ckv_past)
            + jnp.einsum('bhqk,bkc->bqhc', p[..., p_len:], ckv_new))


def setup_inputs(seed: int = 0) -> dict:
    key = jax.random.key(seed)
    ks = iter(jax.random.split(key, 48))
    f32 = jnp.float32

    def nrm(shape, scale):
        return jax.random.normal(next(ks), shape, f32) * scale

    def gain(shape):
        return 1.0 + nrm(shape, 0.01)

    n_pages = PAST_LEN // PAGE_SIZE
    n_pool = (DEC_BATCH * n_pages * 5) // 4
    d = D_MODEL
    inputs = {}
    inputs['x_prompt'] = nrm((BATCH, SEQ, d), 1.0)
    inputs['x_sample'] = nrm((DEC_BATCH, DEC_SEQ, d), 1.0)
    inputs['cache_ckv'] = nrm((N_MLA_LAYERS, n_pool, PAGE_SIZE, MLA_KV_RANK), 1.0)
    inputs['cache_krope'] = nrm((N_MLA_LAYERS, n_pool, PAGE_SIZE, MLA_ROPE), 1.0)
    inputs['state_conv'] = nrm((N_CONV_LAYERS, DEC_BATCH, CONV_WIDTH - 1, d), 0.5)
    perm = jax.random.permutation(next(ks), n_pool)[:DEC_BATCH * n_pages]
    inputs['page_table'] = perm.reshape(DEC_BATCH, n_pages).astype(jnp.int32)
    inputs['c_prompt'] = nrm((BATCH, d), 1.0)
    inputs['c_sample'] = nrm((DEC_BATCH, d), 1.0)
    inputs['ada_w'] = nrm((DEPTH, d, 6 * d), d ** -0.5)
    inputs['ada_b'] = nrm((DEPTH, 6 * d), 0.01)
    inputs['ln_mix_g'] = gain((DEPTH, d))
    inputs['ln_mix_b'] = nrm((DEPTH, d), 0.01)
    inputs['ln_ffn_g'] = gain((DEPTH, d))
    inputs['ln_ffn_b'] = nrm((DEPTH, d), 0.01)
    inputs['ffn_w_gate'] = nrm((DEPTH, d, FFN_DIM), d ** -0.5)
    inputs['ffn_w_up'] = nrm((DEPTH, d, FFN_DIM), d ** -0.5 * BETA)
    inputs['ffn_w_down'] = nrm((DEPTH, FFN_DIM, d), FFN_DIM ** -0.5 * BETA)
    inputs['gmlp_w_in'] = nrm((N_GMLP_LAYERS, d, 2 * GMLP_DIM), d ** -0.5)
    inputs['gmlp_ln_g'] = gain((N_GMLP_LAYERS, GMLP_DIM))
    inputs['gmlp_ln_b'] = nrm((N_GMLP_LAYERS, GMLP_DIM), 0.01)
    inputs['gmlp_w_s'] = nrm((N_GMLP_LAYERS, GMLP_GROUPS, GMLP_CHUNK, GMLP_CHUNK), GMLP_CHUNK ** -0.5)
    inputs['gmlp_b_s'] = 1.0 + nrm((N_GMLP_LAYERS, GMLP_GROUPS, GMLP_CHUNK), 0.1)
    inputs['gmlp_w_out'] = nrm((N_GMLP_LAYERS, GMLP_DIM, d), GMLP_DIM ** -0.5 * BETA)
    inputs['conv_w_pw1'] = nrm((N_CONV_LAYERS, d, 2 * d), d ** -0.5)
    inputs['conv_b_pw1'] = nrm((N_CONV_LAYERS, 2 * d), 0.01)
    inputs['conv_w_dw'] = nrm((N_CONV_LAYERS, CONV_WIDTH, d), CONV_WIDTH ** -0.5)
    inputs['conv_b_dw'] = nrm((N_CONV_LAYERS, d), 0.01)
    inputs['conv_ln_g'] = gain((N_CONV_LAYERS, d))
    inputs['conv_ln_b'] = nrm((N_CONV_LAYERS, d), 0.01)
    inputs['conv_w_pw2'] = nrm((N_CONV_LAYERS, d, d), d ** -0.5 * BETA)
    inputs['conv_b_pw2'] = nrm((N_CONV_LAYERS, d), 0.01)
    inputs['mla_w_down'] = nrm((N_MLA_LAYERS, d, MLA_Q_RANK + MLA_KV_RANK + MLA_ROPE), d ** -0.5)
    inputs['mla_g_q'] = gain((N_MLA_LAYERS, MLA_Q_RANK))
    inputs['mla_g_kv'] = gain((N_MLA_LAYERS, MLA_KV_RANK))
    inputs['mla_w_uq'] = nrm((N_MLA_LAYERS, MLA_Q_RANK, MLA_HEADS * (MLA_NOPE + MLA_ROPE)), MLA_Q_RANK ** -0.5)
    inputs['mla_w_uk'] = nrm((N_MLA_LAYERS, MLA_KV_RANK, MLA_HEADS, MLA_NOPE), MLA_KV_RANK ** -0.5)
    inputs['mla_w_uv'] = nrm((N_MLA_LAYERS, MLA_KV_RANK, MLA_HEADS, MLA_V), MLA_KV_RANK ** -0.5 * BETA)
    inputs['mla_w_o'] = nrm((N_MLA_LAYERS, MLA_HEADS * MLA_V, d), (MLA_HEADS * MLA_V) ** -0.5 * BETA)
    return inputs


def reference(x_prompt, x_sample, cache_ckv, cache_krope, state_conv, page_table, c_prompt, c_sample,
              ada_w, ada_b, ln_mix_g, ln_mix_b, ln_ffn_g, ln_ffn_b, ffn_w_gate, ffn_w_up, ffn_w_down,
              gmlp_w_in, gmlp_ln_g, gmlp_ln_b, gmlp_w_s, gmlp_b_s, gmlp_w_out,
              conv_w_pw1, conv_b_pw1, conv_w_dw, conv_b_dw, conv_ln_g, conv_ln_b, conv_w_pw2, conv_b_pw2,
              mla_w_down, mla_g_q, mla_g_kv, mla_w_uq, mla_w_uk, mla_w_uv, mla_w_o):
    dec_b = x_sample.shape[0]
    past_len = page_table.shape[1] * cache_ckv.shape[2]
    pos_p = jnp.arange(x_prompt.shape[1])
    pos_s = past_len + jnp.arange(x_sample.shape[1])
    xp, xs = x_prompt, x_sample
    ckv_p_rows, kr_p_rows, ckv_s_rows, kr_s_rows = [], [], [], []
    conv_p_states, conv_s_states, gmlp_v_rows = [], [], []
    for i in range(DEPTH):
        kind, j = i % N_MIXERS, i // N_MIXERS
        sh1p, sc1p, g1p, sh2p, sc2p, g2p = adaln(c_prompt, ada_w[i], ada_b[i])
        sh1s, sc1s, g1s, sh2s, sc2s, g2s = adaln(c_sample, ada_w[i], ada_b[i])
        hp = xp * (1 + sc1p) + sh1p
        hs = xs * (1 + sc1s) + sh1s
        if kind == 0:
            op, _ = gmlp_mixer(hp, gmlp_w_in[j], gmlp_ln_g[j], gmlp_ln_b[j], gmlp_w_s[j], gmlp_b_s[j], gmlp_w_out[j])
            os_, v_s = gmlp_mixer(hs, gmlp_w_in[j], gmlp_ln_g[j], gmlp_ln_b[j], gmlp_w_s[j], gmlp_b_s[j], gmlp_w_out[j])
            gmlp_v_rows.append(v_s)
        elif kind == 1:
            ctx0 = jnp.zeros((xp.shape[0], CONV_WIDTH - 1, D_MODEL), xp.dtype)
            op, st_p = conv_module(hp, ctx0, conv_w_pw1[j], conv_b_pw1[j], conv_w_dw[j], conv_b_dw[j],
                                   conv_ln_g[j], conv_ln_b[j], conv_w_pw2[j], conv_b_pw2[j])
            os_, st_s = conv_module(hs, state_conv[j], conv_w_pw1[j], conv_b_pw1[j], conv_w_dw[j], conv_b_dw[j],
                                    conv_ln_g[j], conv_ln_b[j], conv_w_pw2[j], conv_b_pw2[j])
            conv_p_states.append(st_p)
            conv_s_states.append(st_s)
        else:
            qa, qr, ckv, kr = mla_project(hp, pos_p, mla_w_down[j], mla_g_q[j], mla_g_kv[j], mla_w_uq[j], mla_w_uk[j])
            op = mla_out(mla_prompt_attend(qa, qr, ckv, kr), mla_w_uv[j], mla_w_o[j])
            ckv_p_rows.append(ckv)
            kr_p_rows.append(kr)
            qa, qr, ckv, kr = mla_project(hs, pos_s, mla_w_down[j], mla_g_q[j], mla_g_kv[j], mla_w_uq[j], mla_w_uk[j])
            ckv_past = cache_ckv[j, page_table].reshape(dec_b, past_len, MLA_KV_RANK)
            kr_past = cache_krope[j, page_table].reshape(dec_b, past_len, MLA_ROPE)
            os_ = mla_out(mla_sample_attend(qa, qr, ckv, kr, ckv_past, kr_past), mla_w_uv[j], mla_w_o[j])
            ckv_s_rows.append(ckv)
            kr_s_rows.append(kr)
        xp = layer_norm(ALPHA * xp + g1p * op, ln_mix_g[i], ln_mix_b[i])
        xs = layer_norm(ALPHA * xs + g1s * os_, ln_mix_g[i], ln_mix_b[i])
        fp = swiglu_ffn(xp * (1 + sc2p) + sh2p, ffn_w_gate[i], ffn_w_up[i], ffn_w_down[i])
        fs = swiglu_ffn(xs * (1 + sc2s) + sh2s, ffn_w_gate[i], ffn_w_up[i], ffn_w_down[i])
        xp = layer_norm(ALPHA * xp + g2p * fp, ln_ffn_g[i], ln_ffn_b[i])
        xs = layer_norm(ALPHA * xs + g2s * fs, ln_ffn_g[i], ln_ffn_b[i])
    return (xp, xs, jnp.stack(ckv_p_rows), jnp.stack(kr_p_rows), jnp.stack(ckv_s_rows), jnp.stack(kr_s_rows),
            jnp.stack(conv_p_states), jnp.stack(conv_s_states), jnp.stack(gmlp_v_rows))
```

```python
import functools
import math

import jax
import jax.numpy as jnp
from jax import lax
from jax.experimental import pallas as pl
from jax.experimental.pallas import tpu as pltpu

F32 = jnp.float32
BF16 = jnp.bfloat16

D_MODEL = 1024
DEPTH = 4
N_MIXERS = 3
ALPHA = (2.0 * DEPTH) ** 0.25
LN_EPS = 1e-5
RMS_EPS = 1e-6

GMLP_DIM = 2 * D_MODEL
GMLP_GROUPS = 8
GMLP_CHUNK = 128
GMLP_GROUP_WIDTH = GMLP_DIM // GMLP_GROUPS

CONV_WIDTH = 31
CONV_CTX = CONV_WIDTH - 1

MLA_HEADS = 8
MLA_Q_RANK = 384
MLA_KV_RANK = 256
MLA_NOPE = 128
MLA_ROPE = 64
MLA_V = 128
ROPE_THETA = 10000.0
MLA_SCALE = (MLA_NOPE + MLA_ROPE) ** -0.5
MLA_DOWN_PAD = MLA_Q_RANK + MLA_KV_RANK + 128

FFN_DIM = ((8 * D_MODEL + 3 * 256 - 1) // (3 * 256)) * 256

VMEM_LIMIT_BYTES = 56 * 1024 * 1024
LANES = 128
SUBLANES = 8

NEG_BIG = -1e30


def _params(*semantics):
    return pltpu.CompilerParams(dimension_semantics=semantics, vmem_limit_bytes=VMEM_LIMIT_BYTES)


def _dot(a, b):
    return jnp.dot(a, b, preferred_element_type=F32)


def _dot_nt(a, b):
    return lax.dot_general(a, b, (((1,), (1,)), ((), ())), preferred_element_type=F32)


def _layer_norm(x, g, b):
    mu = jnp.mean(x, -1, keepdims=True)
    xc = x - mu
    var = jnp.mean(xc * xc, -1, keepdims=True)
    return xc * lax.rsqrt(var + LN_EPS) * g + b


def _rms_norm(x, g):
    return x * lax.rsqrt(jnp.mean(x * x, -1, keepdims=True) + RMS_EPS) * g


def _mod(mod_ref, k):
    return mod_ref[:, :, k * D_MODEL:(k + 1) * D_MODEL]


def _modulated_rows(x, mod_ref, k_shift, k_scale):
    g, r, d = x.shape
    h = x * (1.0 + _mod(mod_ref, k_scale)) + _mod(mod_ref, k_shift)
    return h.reshape(g * r, d).astype(BF16)


def _residual_norm(x, y_rows, gate, g_ref, b_ref):
    return _layer_norm(ALPHA * x + gate * y_rows.reshape(x.shape), g_ref[...], b_ref[...])


def _ada_kernel(cp_ref, cs_ref, w_ref, b_ref, mp_ref, ms_ref):
    w = w_ref[...].astype(BF16)
    bias = b_ref[...]
    for c_ref, o_ref in ((cp_ref, mp_ref), (cs_ref, ms_ref)):
        c = c_ref[...]
        a = (c * jax.nn.sigmoid(c)).astype(BF16)
        o_ref[...] = _dot(a, w) + bias


def _ada_call(c_p, c_s, ada_w, ada_b):
    tn = 1536
    n_p, n_s = c_p.shape[0], c_s.shape[0]
    d6 = ada_w.shape[-1]
    return pl.pallas_call(
        _ada_kernel,
        grid=(DEPTH, d6 // tn),
        in_specs=[
            pl.BlockSpec((n_p, D_MODEL), lambda l, n: (0, 0)),
            pl.BlockSpec((n_s, D_MODEL), lambda l, n: (0, 0)),
            pl.BlockSpec((None, D_MODEL, tn), lambda l, n: (l, 0, n)),
            pl.BlockSpec((None, 1, tn), lambda l, n: (l, 0, n)),
        ],
        out_specs=[
            pl.BlockSpec((None, n_p, tn), lambda l, n: (l, 0, n)),
            pl.BlockSpec((None, n_s, tn), lambda l, n: (l, 0, n)),
        ],
        out_shape=[
            jax.ShapeDtypeStruct((DEPTH, n_p, d6), F32),
            jax.ShapeDtypeStruct((DEPTH, n_s, d6), F32),
        ],
        compiler_params=_params("arbitrary", "arbitrary"),
        name="ada_table",
    )(c_p, c_s, ada_w, ada_b.reshape(DEPTH, 1, d6))


class _Stream:
    def __init__(self, x, tile_rows):
        self.n_groups, self.group_rows, _ = x.shape
        if self.group_rows >= tile_rows:
            assert self.group_rows % tile_rows == 0
            self.g, self.r = 1, tile_rows
            self.tiles_per_group = self.group_rows // tile_rows
        else:
            assert tile_rows % self.group_rows == 0
            self.g, self.r = tile_rows // self.group_rows, self.group_rows
            assert self.n_groups % self.g == 0
            self.tiles_per_group = 1
        self.rows = self.g * self.r
        self.n_tiles = (self.n_groups // self.g) * self.tiles_per_group
        self.n_tokens = self.n_groups * self.group_rows

    def _gi(self, i):
        if self.tiles_per_group == 1:
            return i, 0
        return i // self.tiles_per_group, i % self.tiles_per_group

    def x_spec(self, width=D_MODEL):
        return pl.BlockSpec((self.g, self.r, width), lambda i: (*self._gi(i), 0))

    def mod_spec(self, layer):
        return pl.BlockSpec((None, self.g, 1, 6 * D_MODEL), lambda i: (layer, self._gi(i)[0], 0, 0))

    def rows_spec(self, width):
        return pl.BlockSpec((self.rows, width), lambda i: (i, 0))

    def x_shape(self, width=D_MODEL, dtype=F32):
        return jax.ShapeDtypeStruct((self.n_groups, self.group_rows, width), dtype)


def _const_spec(shape):
    zeros = (0,) * len(shape)
    return pl.BlockSpec(shape, lambda *_: zeros, pipeline_mode=pl.Buffered(1))


def _layer_spec(arr, layer):
    zeros = (0,) * (arr.ndim - 1)
    return pl.BlockSpec((None,) + arr.shape[1:], lambda *_: (layer,) + zeros,
                        pipeline_mode=pl.Buffered(1))


FFN_CHUNK = FFN_DIM // 2


def _ffn_kernel(x_ref, mod_ref, wg_ref, wu_ref, wd_ref, g_ref, b_ref, o_ref, a_scr):
    x = x_ref[...]
    h = _modulated_rows(x, mod_ref, 3, 4)
    for c in range(FFN_DIM // FFN_CHUNK):
        sl = slice(c * FFN_CHUNK, (c + 1) * FFN_CHUNK)
        gate = _dot(h, wg_ref[:, sl])
        up = _dot(h, wu_ref[:, sl])
        a_scr[:, sl] = (gate * jax.nn.sigmoid(gate) * up).astype(BF16)
    f = _dot(a_scr[...], wd_ref[...])
    o_ref[...] = _residual_norm(x, f, _mod(mod_ref, 5), g_ref, b_ref)


def _ffn_call(x, mod, layer, wg, wu, wd, ln_g, ln_b, tile_rows):
    s = _Stream(x, tile_rows)
    return pl.pallas_call(
        _ffn_kernel,
        grid=(s.n_tiles,),
        in_specs=[s.x_spec(), s.mod_spec(layer), _layer_spec(wg, layer), _layer_spec(wu, layer),
                  _layer_spec(wd, layer), _layer_spec(ln_g, layer), _layer_spec(ln_b, layer)],
        out_specs=s.x_spec(),
        out_shape=s.x_shape(),
        scratch_shapes=[pltpu.VMEM((s.rows, FFN_DIM), BF16)],
        compiler_params=_params("arbitrary"),
        name=f"ffn_l{layer}_g{s.g}",
    )(x, mod, wg, wu, wd, ln_g, ln_b)


def _gelu(z):
    return 0.5 * z * (1.0 + lax.erf(z * (1.0 / math.sqrt(2.0))))


def _gmlp_kernel(x_ref, mod_ref, win_ref, lng_ref, lnb_ref, wmix_ref, bmix_ref, wout_ref, g_ref, b_ref,
                 o_ref, *rest):
    if len(rest) == 2:
        v_ref, um_scr = rest
    else:
        v_ref, (um_scr,) = None, rest
    x = x_ref[...]
    h = _modulated_rows(x, mod_ref, 0, 1)
    rows = h.shape[0]
    z = _gelu(_dot(h, win_ref[...]))
    u = z[:, :GMLP_DIM]
    v = _layer_norm(z[:, GMLP_DIM:], lng_ref[...], lnb_ref[...])
    if v_ref is not None:
        v_ref[...] = v.reshape(v_ref.shape)
    vb = v.astype(BF16)
    row_id = lax.broadcasted_iota(jnp.int32, (GMLP_CHUNK, GMLP_CHUNK), 0)
    col_id = lax.broadcasted_iota(jnp.int32, (GMLP_CHUNK, GMLP_CHUNK), 1)
    causal = row_id >= col_id
    for gi in range(GMLP_GROUPS):
        w = jnp.where(causal, wmix_ref[gi], jnp.zeros((), BF16))
        bias = bmix_ref[:, gi:gi + 1]
        cols = slice(gi * GMLP_GROUP_WIDTH, (gi + 1) * GMLP_GROUP_WIDTH)
        for c in range(rows // GMLP_CHUNK):
            rws = slice(c * GMLP_CHUNK, (c + 1) * GMLP_CHUNK)
            mixed = _dot(w, vb[rws, cols]) + bias
            um_scr[rws, cols] = (u[rws, cols] * mixed).astype(BF16)
    y = _dot(um_scr[...], wout_ref[...])
    o_ref[...] = _residual_norm(x, y, _mod(mod_ref, 2), g_ref, b_ref)


def _gmlp_call(x, mod, layer, j, w_in, ln_g, ln_b, wmix, bmix, w_out, mix_g, mix_b, tile_rows, emit_v):
    s = _Stream(x, tile_rows)
    out_specs = [s.x_spec()]
    out_shape = [s.x_shape()]
    if emit_v:
        out_specs.append(s.x_spec(GMLP_DIM))
        out_shape.append(s.x_shape(GMLP_DIM))
    res = pl.pallas_call(
        _gmlp_kernel,
        grid=(s.n_tiles,),
        in_specs=[s.x_spec(), s.mod_spec(layer), _layer_spec(w_in, j), _layer_spec(ln_g, j),
                  _layer_spec(ln_b, j), _const_spec(wmix.shape), _const_spec(bmix.shape),
                  _layer_spec(w_out, j), _layer_spec(mix_g, layer), _layer_spec(mix_b, layer)],
        out_specs=out_specs,
        out_shape=out_shape,
        scratch_shapes=[pltpu.VMEM((s.rows, GMLP_DIM), BF16)],
        compiler_params=_params("arbitrary"),
        name=f"gmlp_l{layer}_g{s.g}",
    )(x, mod, w_in, ln_g, ln_b, wmix, bmix, w_out, mix_g, mix_b)
    return res if emit_v else (res[0], None)


CONV_HALO = 32
CONV_ROW_CHUNK = 32
CONV_LANE_CHUNK = 512


def _glu_rows(h, wpw1_ref, bpw1_ref):
    a = _dot(h, wpw1_ref[...]) + bpw1_ref[...]
    return a[:, :D_MODEL] * jax.nn.sigmoid(a[:, D_MODEL:])


def _conv_tail(x, y, mod_ref, cg_ref, cb_ref, wpw2_ref, bpw2_ref, g_ref, b_ref):
    y = _layer_norm(y, cg_ref[...], cb_ref[...])
    y = (y * jax.nn.sigmoid(y)).astype(BF16)
    out = _dot(y, wpw2_ref[...]) + bpw2_ref[...]
    return _residual_norm(x, out, _mod(mod_ref, 2), g_ref, b_ref)


def _conv_p_kernel(x_ref, mod_ref, wpw1_ref, bpw1_ref, wdw_ref, bdw_ref, cg_ref, cb_ref, wpw2_ref, bpw2_ref,
                   g_ref, b_ref, o_ref, st_ref, buf_scr, y_scr):
    t = pl.program_id(1)
    x = x_ref[...]
    rows = x.shape[1]
    off = CONV_HALO - CONV_CTX

    @pl.when(t == 0)
    def _():
        buf_scr[0:CONV_HALO, :] = jnp.zeros((CONV_HALO, D_MODEL), F32)

    h = _modulated_rows(x, mod_ref, 0, 1)
    buf_scr[CONV_HALO:CONV_HALO + rows, :] = _glu_rows(h, wpw1_ref, bpw1_ref)

    n_q = (CONV_WIDTH + SUBLANES - 1) // SUBLANES
    span = CONV_ROW_CHUNK + SUBLANES * (n_q - 1)
    for lc in range(D_MODEL // CONV_LANE_CHUNK):
        lanes = slice(lc * CONV_LANE_CHUNK, (lc + 1) * CONV_LANE_CHUNK)
        wdw = wdw_ref[:, lanes]
        bias = bdw_ref[:, lanes]
        for rc in range(rows // CONV_ROW_CHUNK):
            r0 = rc * CONV_ROW_CHUNK
            acc = jnp.broadcast_to(bias, (CONV_ROW_CHUNK, CONV_LANE_CHUNK))
            for r in range(SUBLANES):
                phase = buf_scr[r0 + off + r:r0 + off + r + span, lanes]
                for q in range(n_q):
                    k = SUBLANES * q + r
                    if k < CONV_WIDTH:
                        acc = acc + wdw[k:k + 1, :] * phase[SUBLANES * q:SUBLANES * q + CONV_ROW_CHUNK, :]
            y_scr[r0:r0 + CONV_ROW_CHUNK, lanes] = acc

    @pl.when(t == pl.num_programs(1) - 1)
    def _():
        st_ref[...] = buf_scr[rows + off:rows + CONV_HALO, :].reshape(st_ref.shape)

    buf_scr[0:CONV_HALO, :] = buf_scr[rows:rows + CONV_HALO, :]
    o_ref[...] = _conv_tail(x, y_scr[...], mod_ref, cg_ref, cb_ref, wpw2_ref, bpw2_ref, g_ref, b_ref)


def _conv_p_call(x, mod, layer, j, w_pw1, b_pw1, w_dw, b_dw, cln_g, cln_b, w_pw2, b_pw2, mix_g, mix_b,
                 tile_rows):
    n_b, seq, _ = x.shape
    n_t = seq // tile_rows
    xspec = pl.BlockSpec((1, tile_rows, D_MODEL), lambda b, t: (b, t, 0))
    return pl.pallas_call(
        _conv_p_kernel,
        grid=(n_b, n_t),
        in_specs=[xspec,
                  pl.BlockSpec((None, 1, 1, 6 * D_MODEL), lambda b, t: (layer, b, 0, 0)),
                  _layer_spec(w_pw1, j), _layer_spec(b_pw1, j), _layer_spec(w_dw, j), _layer_spec(b_dw, j),
                  _layer_spec(cln_g, j), _layer_spec(cln_b, j), _layer_spec(w_pw2, j), _layer_spec(b_pw2, j),
                  _layer_spec(mix_g, layer), _layer_spec(mix_b, layer)],
        out_specs=[xspec, pl.BlockSpec((1, CONV_CTX, D_MODEL), lambda b, t: (b, 0, 0))],
        out_shape=[jax.ShapeDtypeStruct(x.shape, F32),
                   jax.ShapeDtypeStruct((n_b, CONV_CTX, D_MODEL), F32)],
        scratch_shapes=[pltpu.VMEM((tile_rows + CONV_HALO, D_MODEL), F32),
                        pltpu.VMEM((tile_rows, D_MODEL), F32)],
        compiler_params=_params("arbitrary", "arbitrary"),
        name=f"conv_p_l{layer}",
    )(x, mod, w_pw1, b_pw1, w_dw, b_dw, cln_g, cln_b, w_pw2, b_pw2, mix_g, mix_b)


def _conv_s_kernel(x_ref, mod_ref, st_ref, wpw1_ref, bpw1_ref, coef_ref, bdw_ref, cg_ref, cb_ref,
                   wpw2_ref, bpw2_ref, g_ref, b_ref, o_ref, a_ref):
    x = x_ref[...]
    g, r, _ = x.shape
    h = _modulated_rows(x, mod_ref, 0, 1)
    a = _glu_rows(h, wpw1_ref, bpw1_ref).reshape(g, r, D_MODEL)
    a_ref[...] = a
    y = jnp.broadcast_to(bdw_ref[...], (g, r, D_MODEL))
    for i in range(CONV_CTX):
        y = y + coef_ref[i] * st_ref[:, i:i + 1, :]
    for i in range(r):
        y = y + coef_ref[CONV_CTX + i] * a[:, i:i + 1, :]
    o_ref[...] = _conv_tail(x, y.reshape(g * r, D_MODEL), mod_ref, cg_ref, cb_ref, wpw2_ref, bpw2_ref,
                            g_ref, b_ref)


def _conv_s_call(x, mod, layer, j, state, w_pw1, b_pw1, coef, b_dw, cln_g, cln_b, w_pw2, b_pw2, mix_g, mix_b,
                 tile_rows):
    s = _Stream(x, tile_rows)
    st_spec = pl.BlockSpec((None, s.g, CONV_CTX, D_MODEL), lambda i: (j, i, 0, 0))
    return pl.pallas_call(
        _conv_s_kernel,
        grid=(s.n_tiles,),
        in_specs=[s.x_spec(), s.mod_spec(layer), st_spec, _layer_spec(w_pw1, j), _layer_spec(b_pw1, j),
                  _const_spec(coef.shape), _layer_spec(b_dw, j), _layer_spec(cln_g, j), _layer_spec(cln_b, j),
                  _layer_spec(w_pw2, j), _layer_spec(b_pw2, j), _layer_spec(mix_g, layer),
                  _layer_spec(mix_b, layer)],
        out_specs=[s.x_spec(), s.x_spec()],
        out_shape=[s.x_shape(), s.x_shape()],
        compiler_params=_params("arbitrary"),
        name=f"conv_s_l{layer}",
    )(x, mod, state, w_pw1, b_pw1, coef, b_dw, cln_g, cln_b, w_pw2, b_pw2, mix_g, mix_b)


def _rope_tile(x, cos, sin_signed):
    lane = lax.broadcasted_iota(jnp.int32, x.shape, 1)
    upper = pltpu.roll(x, LANES - MLA_ROPE // 2, 1)
    lower = pltpu.roll(x, MLA_ROPE // 2, 1)
    swapped = jnp.where((lane % MLA_ROPE) < MLA_ROPE // 2, upper, lower)
    return x * cos + swapped * sin_signed


def _mla_proj_kernel(x_ref, mod_ref, wdown_ref, gq_ref, gkv_ref, wuq_ref, wuk_ref, cos_ref, sin_ref,
                     qa_ref, qr_ref, ckv_ref, kr_ref, *bf_refs):
    x = x_ref[...]
    h = _modulated_rows(x, mod_ref, 0, 1)
    d = _dot(h, wdown_ref[...])
    q_lat = _rms_norm(d[:, :MLA_Q_RANK], gq_ref[...])
    ckv = _rms_norm(d[:, MLA_Q_RANK:MLA_Q_RANK + MLA_KV_RANK], gkv_ref[...])
    cos = cos_ref[...]
    sin = sin_ref[...]
    kr = _rope_tile(d[:, MLA_Q_RANK + MLA_KV_RANK:], cos, sin)[:, :MLA_ROPE]
    ckv_ref[...] = ckv
    kr_ref[...] = kr
    if bf_refs:
        bf_refs[0][...] = ckv.astype(BF16)
        bf_refs[1][...] = kr.astype(BF16)
    q = _dot(q_lat.astype(BF16), wuq_ref[...])
    n_nope = MLA_HEADS * MLA_NOPE
    for hd in range(MLA_HEADS):
        qn = q[:, hd * MLA_NOPE:(hd + 1) * MLA_NOPE].astype(BF16)
        qa_ref[hd] = (_dot(qn, wuk_ref[hd]) * MLA_SCALE).astype(qa_ref.dtype)
    for t in range(MLA_HEADS * MLA_ROPE // LANES):
        rot = _rope_tile(q[:, n_nope + t * LANES:n_nope + (t + 1) * LANES], cos, sin) * MLA_SCALE
        qr_ref[2 * t] = rot[:, :MLA_ROPE].astype(qr_ref.dtype)
        qr_ref[2 * t + 1] = rot[:, MLA_ROPE:].astype(qr_ref.dtype)


def _mla_proj_call(x, mod, layer, j, w_down, g_q, g_kv, w_uq, w_uk, cos, sin, tile_rows, q_dtype, emit_bf16,
                   pos_tiles):
    s = _Stream(x, tile_rows)
    n = s.n_tokens
    if pos_tiles == 1:
        pos_spec = pl.BlockSpec((s.rows, LANES), lambda i: (0, 0))
    else:
        pos_spec = pl.BlockSpec((s.rows, LANES), lambda i: (i % pos_tiles, 0))
    out_specs = [pl.BlockSpec((MLA_HEADS, s.rows, MLA_KV_RANK), lambda i: (0, i, 0)),
                 pl.BlockSpec((MLA_HEADS, s.rows, MLA_ROPE), lambda i: (0, i, 0)),
                 s.rows_spec(MLA_KV_RANK), s.rows_spec(MLA_ROPE)]
    out_shape = [jax.ShapeDtypeStruct((MLA_HEADS, n, MLA_KV_RANK), q_dtype),
                 jax.ShapeDtypeStruct((MLA_HEADS, n, MLA_ROPE), q_dtype),
                 jax.ShapeDtypeStruct((n, MLA_KV_RANK), F32),
                 jax.ShapeDtypeStruct((n, MLA_ROPE), F32)]
    if emit_bf16:
        out_specs += [s.rows_spec(MLA_KV_RANK), s.rows_spec(MLA_ROPE)]
        out_shape += [jax.ShapeDtypeStruct((n, MLA_KV_RANK), BF16),
                      jax.ShapeDtypeStruct((n, MLA_ROPE), BF16)]
    return pl.pallas_call(
        _mla_proj_kernel,
        grid=(s.n_tiles,),
        in_specs=[s.x_spec(), s.mod_spec(layer), _layer_spec(w_down, j), _layer_spec(g_q, j),
                  _layer_spec(g_kv, j), _layer_spec(w_uq, j), _layer_spec(w_uk, j), pos_spec, pos_spec],
        out_specs=out_specs,
        out_shape=out_shape,
        compiler_params=_params("arbitrary"),
        name=f"mla_proj_l{layer}_g{s.g}",
    )(x, mod, w_down, g_q, g_kv, w_uq, w_uk, cos, sin)


ATTN_Q_TILE = 256
ATTN_K_TILE = 256


def _attn_p_kernel(qa_ref, qr_ref, k_ref, r_ref, o_ref, m_scr, l_scr, acc_scr):
    i = pl.program_id(1)
    heads, tq, _ = qa_ref.shape
    rows = heads * tq
    qa = qa_ref[...].reshape(rows, MLA_KV_RANK)
    qr = qr_ref[...].reshape(rows, MLA_ROPE)
    m_scr[...] = jnp.full(m_scr.shape, NEG_BIG, F32)
    l_scr[...] = jnp.zeros(l_scr.shape, F32)
    acc_scr[...] = jnp.zeros(acc_scr.shape, F32)

    def step(jb, masked):
        k0 = pl.multiple_of(jb * ATTN_K_TILE, ATTN_K_TILE)
        kb = k_ref[pl.ds(k0, ATTN_K_TILE), :]
        rb = r_ref[pl.ds(k0, ATTN_K_TILE), :]
        s = _dot_nt(qa, kb) + _dot_nt(qr, rb)
        if masked:
            q_pos = lax.broadcasted_iota(jnp.int32, s.shape, 0) % tq
            k_pos = lax.broadcasted_iota(jnp.int32, s.shape, 1)
            s = jnp.where(k_pos <= q_pos, s, NEG_BIG)
        m_prev = m_scr[...]
        m_next = jnp.maximum(m_prev, jnp.max(s, axis=1, keepdims=True))
        p = jnp.exp(s - jnp.concatenate([m_next] * (ATTN_K_TILE // LANES), axis=1))
        alpha = jnp.exp(m_prev - m_next)
        l_scr[...] = alpha * l_scr[...] + jnp.sum(p, axis=1, keepdims=True)
        m_scr[...] = m_next
        acc_scr[...] = (acc_scr[...] * jnp.concatenate([alpha] * (MLA_KV_RANK // LANES), axis=1)
                        + _dot(p.astype(BF16), kb))

    def body(jb, carry):
        step(jb, False)
        return carry

    lax.fori_loop(0, i, body, 0)
    step(i, True)
    inv = 1.0 / l_scr[...]
    ctx = acc_scr[...] * jnp.concatenate([inv] * (MLA_KV_RANK // LANES), axis=1)
    for hd in range(heads):
        o_ref[:, hd * MLA_KV_RANK:(hd + 1) * MLA_KV_RANK] = ctx[hd * tq:(hd + 1) * tq].astype(o_ref.dtype)


def _attn_p_call(qa, qr, ckv_b, kr_b, n_batch, seq):
    assert ATTN_Q_TILE == ATTN_K_TILE
    nq = seq // ATTN_Q_TILE
    rows = MLA_HEADS * ATTN_Q_TILE
    return pl.pallas_call(
        _attn_p_kernel,
        grid=(n_batch, nq),
        in_specs=[pl.BlockSpec((MLA_HEADS, ATTN_Q_TILE, MLA_KV_RANK), lambda b, i: (0, b * nq + i, 0)),
                  pl.BlockSpec((MLA_HEADS, ATTN_Q_TILE, MLA_ROPE), lambda b, i: (0, b * nq + i, 0)),
                  pl.BlockSpec((seq, MLA_KV_RANK), lambda b, i: (b, 0)),
                  pl.BlockSpec((seq, MLA_ROPE), lambda b, i: (b, 0))],
        out_specs=pl.BlockSpec((ATTN_Q_TILE, MLA_HEADS * MLA_KV_RANK), lambda b, i: (b * nq + i, 0)),
        out_shape=jax.ShapeDtypeStruct((n_batch * seq, MLA_HEADS * MLA_KV_RANK), BF16),
        scratch_shapes=[pltpu.VMEM((rows, LANES), F32), pltpu.VMEM((rows, LANES), F32),
                        pltpu.VMEM((rows, MLA_KV_RANK), F32)],
        compiler_params=_params("arbitrary", "arbitrary"),
        name="mla_attn_p",
    )(qa, qr, ckv_b, kr_b)


def _attn_s_kernel(j, pt_ref, qa_ref, qr_ref, kn_ref, rn_ref, cache_k, cache_r, o_ref, kbuf, rbuf, sems):
    b = pl.program_id(0)
    nb = pl.num_programs(0)
    n_pages = pt_ref.shape[1]
    page = cache_k.shape[2]
    slot = b % 2

    def copies(bi, sl, p):
        pg = pt_ref[bi, p]
        dst = pl.ds(pl.multiple_of(p * page, page), page)
        return (pltpu.make_async_copy(cache_k.at[j, pg], kbuf.at[sl, dst], sems.at[0, sl]),
                pltpu.make_async_copy(cache_r.at[j, pg], rbuf.at[sl, dst], sems.at[1, sl]))

    def start_all(bi, sl):
        def body(p, c):
            ck, cr = copies(bi, sl, p)
            ck.start()
            cr.start()
            return c
        lax.fori_loop(0, n_pages, body, 0)

    def wait_all(bi, sl):
        def body(p, c):
            ck, cr = copies(bi, sl, p)
            ck.wait()
            cr.wait()
            return c
        lax.fori_loop(0, n_pages, body, 0)

    @pl.when(b == 0)
    def _():
        start_all(0, 0)

    @pl.when(b + 1 < nb)
    def _():
        start_all(b + 1, 1 - slot)

    wait_all(b, slot)

    heads, t, _ = qa_ref.shape
    rows = heads * t
    qa = qa_ref[...].reshape(rows, MLA_KV_RANK).astype(BF16)
    qr = qr_ref[...].reshape(rows, MLA_ROPE).astype(BF16)
    kp = kbuf[slot].astype(BF16)
    rp = rbuf[slot].astype(BF16)
    kn = kn_ref[...].astype(BF16)
    rn = rn_ref[...].astype(BF16)
    s_past = _dot_nt(qa, kp) + _dot_nt(qr, rp)
    s_new = _dot_nt(qa, kn) + _dot_nt(qr, rn)
    q_pos = lax.broadcasted_iota(jnp.int32, s_new.shape, 0) % t
    k_pos = lax.broadcasted_iota(jnp.int32, s_new.shape, 1)
    s_new = jnp.where(k_pos <= q_pos, s_new, NEG_BIG)
    m = jnp.maximum(jnp.max(s_past, axis=1, keepdims=True), jnp.max(s_new, axis=1, keepdims=True))
    p_past = jnp.exp(s_past - m)
    p_new = jnp.exp(s_new - m)
    denom = jnp.sum(p_past, axis=1, keepdims=True) + jnp.sum(p_new, axis=1, keepdims=True)
    ctx = (_dot(p_past.astype(BF16), kp) + _dot(p_new.astype(BF16), kn)) / denom
    for hd in range(heads):
        o_ref[:, hd * MLA_KV_RANK:(hd + 1) * MLA_KV_RANK] = ctx[hd * t:(hd + 1) * t]


def _attn_s_call(page_table, qa, qr, ckv_new, kr_new, cache_k, cache_r, j, n_batch, t):
    past = page_table.shape[1] * cache_k.shape[2]
    grid_spec = pltpu.PrefetchScalarGridSpec(
        num_scalar_prefetch=1,
        grid=(n_batch,),
        in_specs=[pl.BlockSpec((MLA_HEADS, t, MLA_KV_RANK), lambda b, pt: (0, b, 0)),
                  pl.BlockSpec((MLA_HEADS, t, MLA_ROPE), lambda b, pt: (0, b, 0)),
                  pl.BlockSpec((t, MLA_KV_RANK), lambda b, pt: (b, 0)),
                  pl.BlockSpec((t, MLA_ROPE), lambda b, pt: (b, 0)),
                  pl.BlockSpec(memory_space=pl.ANY),
                  pl.BlockSpec(memory_space=pl.ANY)],
        out_specs=pl.BlockSpec((t, MLA_HEADS * MLA_KV_RANK), lambda b, pt: (b, 0)),
        scratch_shapes=[pltpu.VMEM((2, past, MLA_KV_RANK), F32),
                        pltpu.VMEM((2, past, MLA_ROPE), F32),
                        pltpu.SemaphoreType.DMA((2, 2))],
    )
    return pl.pallas_call(
        functools.partial(_attn_s_kernel, j),
        grid_spec=grid_spec,
        out_shape=jax.ShapeDtypeStruct((n_batch * t, MLA_HEADS * MLA_KV_RANK), F32),
        compiler_params=_params("arbitrary"),
        name="mla_attn_s",
    )(page_table, qa, qr, ckv_new, kr_new, cache_k, cache_r)


def _mla_out_kernel(x_ref, mod_ref, ctx_ref, wuv_ref, wo_ref, g_ref, b_ref, o_ref):
    x = x_ref[...]
    parts = []
    for hd in range(MLA_HEADS):
        c = ctx_ref[:, hd * MLA_KV_RANK:(hd + 1) * MLA_KV_RANK].astype(BF16)
        parts.append(_dot(c, wuv_ref[hd]).astype(BF16))
    o = jnp.concatenate(parts, axis=1)
    y = _dot(o, wo_ref[...])
    o_ref[...] = _residual_norm(x, y, _mod(mod_ref, 2), g_ref, b_ref)


def _mla_out_call(x, mod, layer, j, ctx, w_uv, w_o, mix_g, mix_b, tile_rows):
    s = _Stream(x, tile_rows)
    return pl.pallas_call(
        _mla_out_kernel,
        grid=(s.n_tiles,),
        in_specs=[s.x_spec(), s.mod_spec(layer), s.rows_spec(MLA_HEADS * MLA_KV_RANK),
                  _layer_spec(w_uv, j), _layer_spec(w_o, j), _layer_spec(mix_g, layer),
                  _layer_spec(mix_b, layer)],
        out_specs=s.x_spec(),
        out_shape=s.x_shape(),
        compiler_params=_params("arbitrary"),
        name=f"mla_out_l{layer}_g{s.g}",
    )(x, mod, ctx, w_uv, w_o, mix_g, mix_b)


def _row3(a):
    return a.reshape(a.shape[0], 1, a.shape[1])


def _rope_tables(pos):
    half = MLA_ROPE // 2
    inv = ROPE_THETA ** (-jnp.arange(half, dtype=F32) / half)
    ang = pos.astype(F32)[:, None] * inv[None, :]
    cos, sin = jnp.cos(ang), jnp.sin(ang)
    cos64 = jnp.concatenate([cos, cos], -1)
    sin64 = jnp.concatenate([-sin, sin], -1)
    return jnp.tile(cos64, (1, LANES // MLA_ROPE)), jnp.tile(sin64, (1, LANES // MLA_ROPE))


def kernel(x_prompt, x_sample, cache_ckv, cache_krope, state_conv, page_table, c_prompt, c_sample,
           ada_w, ada_b, ln_mix_g, ln_mix_b, ln_ffn_g, ln_ffn_b, ffn_w_gate, ffn_w_up, ffn_w_down,
           gmlp_w_in, gmlp_ln_g, gmlp_ln_b, gmlp_w_s, gmlp_b_s, gmlp_w_out,
           conv_w_pw1, conv_b_pw1, conv_w_dw, conv_b_dw, conv_ln_g, conv_ln_b, conv_w_pw2, conv_b_pw2,
           mla_w_down, mla_g_q, mla_g_kv, mla_w_uq, mla_w_uk, mla_w_uv, mla_w_o):
    n_b, seq, _ = x_prompt.shape
    n_db, dec_seq, _ = x_sample.shape
    past_len = page_table.shape[1] * cache_ckv.shape[2]

    c_p = jnp.pad(c_prompt, ((0, SUBLANES - n_b), (0, 0)))
    mod_p, mod_s = _ada_call(c_p, c_sample, ada_w, ada_b)
    mod_p = mod_p.reshape(DEPTH, SUBLANES, 1, 6 * D_MODEL)
    mod_s = mod_s.reshape(DEPTH, n_db, 1, 6 * D_MODEL)

    ln_mix_g, ln_mix_b, ln_ffn_g, ln_ffn_b = map(_row3, (ln_mix_g, ln_mix_b, ln_ffn_g, ln_ffn_b))
    wg, wu, wd = (w.astype(BF16) for w in (ffn_w_gate, ffn_w_up, ffn_w_down))

    g_w_in, g_w_out = gmlp_w_in.astype(BF16), gmlp_w_out.astype(BF16)
    g_ln_g, g_ln_b = _row3(gmlp_ln_g), _row3(gmlp_ln_b)
    reps = GMLP_CHUNK // dec_seq
    eye = jnp.eye(reps, dtype=F32)

    c_w_pw1, c_w_pw2 = conv_w_pw1.astype(BF16), conv_w_pw2.astype(BF16)
    c_b_pw1, c_b_dw, c_ln_g, c_ln_b, c_b_pw2 = map(_row3, (conv_b_pw1, conv_b_dw, conv_ln_g, conv_ln_b,
                                                            conv_b_pw2))
    tap = jnp.arange(CONV_CTX + dec_seq)[:, None] - jnp.arange(dec_seq)[None, :]
    tap_ok = (tap >= 0) & (tap < CONV_WIDTH)
    conv_coef = jnp.where(tap_ok[None, :, :, None], conv_w_dw[:, jnp.clip(tap, 0, CONV_WIDTH - 1)], 0.0)

    n_mla = mla_w_down.shape[0]
    m_w_down = jnp.pad(mla_w_down, ((0, 0), (0, 0), (0, MLA_DOWN_PAD - mla_w_down.shape[-1]))).astype(BF16)
    w_uq = mla_w_uq.reshape(n_mla, MLA_Q_RANK, MLA_HEADS, MLA_NOPE + MLA_ROPE)
    m_w_uq = jnp.concatenate([w_uq[..., :MLA_NOPE].reshape(n_mla, MLA_Q_RANK, -1),
                              w_uq[..., MLA_NOPE:].reshape(n_mla, MLA_Q_RANK, -1)], -1).astype(BF16)
    m_w_uk = mla_w_uk.transpose(0, 2, 3, 1).astype(BF16)
    m_w_uv = mla_w_uv.transpose(0, 2, 1, 3).astype(BF16)
    m_w_o = mla_w_o.astype(BF16)
    m_g_q, m_g_kv = _row3(mla_g_q), _row3(mla_g_kv)
    cos_p, sin_p = _rope_tables(jnp.arange(seq))
    cos_s, sin_s = _rope_tables(past_len + jnp.arange(dec_seq))

    tile_p, tile_s = 512, 512
    gm_tile = 256
    conv_tile = 256
    mla_tile = 512
    s_reps = mla_tile // dec_seq
    cos_s, sin_s = jnp.tile(cos_s, (s_reps, 1)), jnp.tile(sin_s, (s_reps, 1))

    xp, xs = x_prompt, x_sample
    ckv_p_rows, kr_p_rows, ckv_s_rows, kr_s_rows = [], [], [], []
    conv_p_states, conv_s_states, gmlp_v_rows = [], [], []
    for i in range(DEPTH):
        kind, j = i % N_MIXERS, i // N_MIXERS
        if kind == 0:
            wmix_p = gmlp_w_s[j].astype(BF16)
            bmix_p = gmlp_b_s[j].T
            corner = gmlp_w_s[j, :, :dec_seq, :dec_seq]
            wmix_s = jnp.einsum('ab,gts->gatbs', eye, corner).reshape(
                GMLP_GROUPS, GMLP_CHUNK, GMLP_CHUNK).astype(BF16)
            bmix_s = jnp.tile(gmlp_b_s[j, :, :dec_seq].T, (reps, 1))
            xp, _ = _gmlp_call(xp, mod_p, i, j, g_w_in, g_ln_g, g_ln_b, wmix_p, bmix_p, g_w_out,
                               ln_mix_g, ln_mix_b, gm_tile, False)
            xs, v_s = _gmlp_call(xs, mod_s, i, j, g_w_in, g_ln_g, g_ln_b, wmix_s, bmix_s, g_w_out,
                                 ln_mix_g, ln_mix_b, gm_tile, True)
            gmlp_v_rows.append(v_s)
        elif kind == 1:
            xp, st_p = _conv_p_call(xp, mod_p, i, j, c_w_pw1, c_b_pw1, conv_w_dw, c_b_dw, c_ln_g, c_ln_b,
                                    c_w_pw2, c_b_pw2, ln_mix_g, ln_mix_b, conv_tile)
            xs_new, a_s = _conv_s_call(xs, mod_s, i, j, state_conv, c_w_pw1, c_b_pw1, conv_coef[j], c_b_dw,
                                       c_ln_g, c_ln_b, c_w_pw2, c_b_pw2, ln_mix_g, ln_mix_b, conv_tile)
            xs = xs_new
            conv_p_states.append(st_p)
            conv_s_states.append(jnp.concatenate([state_conv[j][:, dec_seq:], a_s], axis=1))
        else:
            qa, qr, ckv, kr, ckv_b, kr_b = _mla_proj_call(
                xp, mod_p, i, j, m_w_down, m_g_q, m_g_kv, m_w_uq, m_w_uk, cos_p, sin_p, mla_tile, BF16,
                True, seq // mla_tile)
            ctx_p = _attn_p_call(qa, qr, ckv_b, kr_b, n_b, seq)
            xp = _mla_out_call(xp, mod_p, i, j, ctx_p, m_w_uv, m_w_o, ln_mix_g, ln_mix_b, mla_tile)
            ckv_p_rows.append(ckv.reshape(n_b, seq, MLA_KV_RANK))
            kr_p_rows.append(kr.reshape(n_b, seq, MLA_ROPE))
            qa, qr, ckv, kr = _mla_proj_call(
                xs, mod_s, i, j, m_w_down, m_g_q, m_g_kv, m_w_uq, m_w_uk, cos_s, sin_s, mla_tile, F32,
                False, 1)
            ctx_s = _attn_s_call(page_table, qa, qr, ckv, kr, cache_ckv, cache_krope, j, n_db, dec_seq)
            xs = _mla_out_call(xs, mod_s, i, j, ctx_s, m_w_uv, m_w_o, ln_mix_g, ln_mix_b, mla_tile)
            ckv_s_rows.append(ckv.reshape(n_db, dec_seq, MLA_KV_RANK))
            kr_s_rows.append(kr.reshape(n_db, dec_seq, MLA_ROPE))
        xp = _ffn_call(xp, mod_p, i, wg, wu, wd, ln_ffn_g, ln_ffn_b, tile_p)
        xs = _ffn_call(xs, mod_s, i, wg, wu, wd, ln_ffn_g, ln_ffn_b, tile_s)
    return (xp, xs, jnp.stack(ckv_p_rows), jnp.stack(kr_p_rows), jnp.stack(ckv_s_rows), jnp.stack(kr_s_rows),
            jnp.stack(conv_p_states), jnp.stack(conv_s_states), jnp.stack(gmlp_v_rows))
```

```python
import functools
import math

import jax
import jax.numpy as jnp
from jax import lax
from jax.experimental import pallas as pl
from jax.experimental.pallas import tpu as pltpu

F32 = jnp.float32
BF16 = jnp.bfloat16

D_MODEL = 1024
DEPTH = 4
N_MIXERS = 3
ALPHA = (2.0 * DEPTH) ** 0.25
LN_EPS = 1e-5
RMS_EPS = 1e-6

GMLP_DIM = 2 * D_MODEL
GMLP_GROUPS = 8
GMLP_CHUNK = 128
GMLP_GROUP_WIDTH = GMLP_DIM // GMLP_GROUPS

CONV_WIDTH = 31
CONV_CTX = CONV_WIDTH - 1

MLA_HEADS = 8
MLA_Q_RANK = 384
MLA_KV_RANK = 256
MLA_NOPE = 128
MLA_ROPE = 64
MLA_V = 128
ROPE_THETA = 10000.0
MLA_SCALE = (MLA_NOPE + MLA_ROPE) ** -0.5
MLA_DOWN_PAD = MLA_Q_RANK + MLA_KV_RANK + 128

FFN_DIM = ((8 * D_MODEL + 3 * 256 - 1) // (3 * 256)) * 256

VMEM_LIMIT_BYTES = 56 * 1024 * 1024
LANES = 128
SUBLANES = 8

NEG_BIG = -1e30


def _params(*semantics):
    return pltpu.CompilerParams(dimension_semantics=semantics, vmem_limit_bytes=VMEM_LIMIT_BYTES)


def _dot(a, b):
    return jnp.dot(a, b, preferred_element_type=F32)


def _dot_nt(a, b):
    return lax.dot_general(a, b, (((1,), (1,)), ((), ())), preferred_element_type=F32)


def _layer_norm(x, g, b):
    mu = jnp.mean(x, -1, keepdims=True)
    xc = x - mu
    var = jnp.mean(xc * xc, -1, keepdims=True)
    return xc * lax.rsqrt(var + LN_EPS) * g + b


def _rms_norm(x, g):
    return x * lax.rsqrt(jnp.mean(x * x, -1, keepdims=True) + RMS_EPS) * g


def _mod(mod_ref, k):
    return mod_ref[:, :, k * D_MODEL:(k + 1) * D_MODEL]


def _modulated_rows(x, mod_ref, k_shift, k_scale):
    g, r, d = x.shape
    h = x * (1.0 + _mod(mod_ref, k_scale)) + _mod(mod_ref, k_shift)
    return h.reshape(g * r, d).astype(BF16)


def _residual_norm(x, y_rows, gate, g_ref, b_ref):
    return _layer_norm(ALPHA * x + gate * y_rows.reshape(x.shape), g_ref[...], b_ref[...])


def _ada_kernel(cp_ref, cs_ref, w_ref, b_ref, mp_ref, ms_ref):
    w = w_ref[...].astype(BF16)
    bias = b_ref[...]
    for c_ref, o_ref in ((cp_ref, mp_ref), (cs_ref, ms_ref)):
        c = c_ref[...]
        a = (c * jax.nn.sigmoid(c)).astype(BF16)
        o_ref[...] = _dot(a, w) + bias


def _ada_call(c_p, c_s, ada_w, ada_b):
    tn = 1536
    n_p, n_s = c_p.shape[0], c_s.shape[0]
    d6 = ada_w.shape[-1]
    return pl.pallas_call(
        _ada_kernel,
        grid=(DEPTH, d6 // tn),
        in_specs=[
            pl.BlockSpec((n_p, D_MODEL), lambda l, n: (0, 0)),
            pl.BlockSpec((n_s, D_MODEL), lambda l, n: (0, 0)),
            pl.BlockSpec((None, D_MODEL, tn), lambda l, n: (l, 0, n)),
            pl.BlockSpec((None, 1, tn), lambda l, n: (l, 0, n)),
        ],
        out_specs=[
            pl.BlockSpec((None, n_p, tn), lambda l, n: (l, 0, n)),
            pl.BlockSpec((None, n_s, tn), lambda l, n: (l, 0, n)),
        ],
        out_shape=[
            jax.ShapeDtypeStruct((DEPTH, n_p, d6), F32),
            jax.ShapeDtypeStruct((DEPTH, n_s, d6), F32),
        ],
        compiler_params=_params("arbitrary", "arbitrary"),
        name="ada_table",
    )(c_p, c_s, ada_w, ada_b.reshape(DEPTH, 1, d6))


class _Stream:
    def __init__(self, x, tile_rows):
        self.n_groups, self.group_rows, _ = x.shape
        if self.group_rows >= tile_rows:
            assert self.group_rows % tile_rows == 0
            self.g, self.r = 1, tile_rows
            self.tiles_per_group = self.group_rows // tile_rows
        else:
            assert tile_rows % self.group_rows == 0
            self.g, self.r = tile_rows // self.group_rows, self.group_rows
            assert self.n_groups % self.g == 0
            self.tiles_per_group = 1
        self.rows = self.g * self.r
        self.n_tiles = (self.n_groups // self.g) * self.tiles_per_group
        self.n_tokens = self.n_groups * self.group_rows

    def _gi(self, i):
        if self.tiles_per_group == 1:
            return i, 0
        return i // self.tiles_per_group, i % self.tiles_per_group

    def x_spec(self, width=D_MODEL):
        return pl.BlockSpec((self.g, self.r, width), lambda i: (*self._gi(i), 0))

    def mod_spec(self, layer):
        return pl.BlockSpec((None, self.g, 1, 6 * D_MODEL), lambda i: (layer, self._gi(i)[0], 0, 0))

    def rows_spec(self, width):
        return pl.BlockSpec((self.rows, width), lambda i: (i, 0))

    def x_shape(self, width=D_MODEL, dtype=F32):
        return jax.ShapeDtypeStruct((self.n_groups, self.group_rows, width), dtype)


def _const_spec(shape):
    zeros = (0,) * len(shape)
    return pl.BlockSpec(shape, lambda *_: zeros, pipeline_mode=pl.Buffered(1))


def _layer_spec(arr, layer):
    zeros = (0,) * (arr.ndim - 1)
    return pl.BlockSpec((None,) + arr.shape[1:], lambda *_: (layer,) + zeros,
                        pipeline_mode=pl.Buffered(1))


MXU_WIDTH = 256
FFN_SPLIT = (FFN_DIM // MXU_WIDTH + 1) // 2 * MXU_WIDTH


def _ffn_kernel(x_ref, mod_ref, wg_ref, wu_ref, wd_ref, g_ref, b_ref, o_ref, a_scr):
    x = x_ref[...]
    h = _modulated_rows(x, mod_ref, 3, 4)
    for sl in (slice(0, FFN_SPLIT), slice(FFN_SPLIT, FFN_DIM)):
        gate = _dot(h, wg_ref[:, sl])
        up = _dot(h, wu_ref[:, sl])
        a_scr[:, sl] = (gate * jax.nn.sigmoid(gate) * up).astype(BF16)
    f = _dot(a_scr[...], wd_ref[...])
    o_ref[...] = _residual_norm(x, f, _mod(mod_ref, 5), g_ref, b_ref)


def _ffn_call(x, mod, layer, wg, wu, wd, ln_g, ln_b, tile_rows):
    s = _Stream(x, tile_rows)
    return pl.pallas_call(
        _ffn_kernel,
        grid=(s.n_tiles,),
        in_specs=[s.x_spec(), s.mod_spec(layer), _layer_spec(wg, layer), _layer_spec(wu, layer),
                  _layer_spec(wd, layer), _layer_spec(ln_g, layer), _layer_spec(ln_b, layer)],
        out_specs=s.x_spec(),
        out_shape=s.x_shape(),
        scratch_shapes=[pltpu.VMEM((s.rows, FFN_DIM), BF16)],
        compiler_params=_params("arbitrary"),
        name=f"ffn_l{layer}_g{s.g}",
    )(x, mod, wg, wu, wd, ln_g, ln_b)


def _gelu(z):
    return 0.5 * z * (1.0 + lax.erf(z * (1.0 / math.sqrt(2.0))))


def _gmlp_kernel(x_ref, mod_ref, win_ref, lng_ref, lnb_ref, wmix_ref, bmix_ref, wout_ref, g_ref, b_ref,
                 o_ref, *rest):
    if len(rest) == 2:
        v_ref, um_scr = rest
    else:
        v_ref, (um_scr,) = None, rest
    x = x_ref[...]
    h = _modulated_rows(x, mod_ref, 0, 1)
    rows = h.shape[0]
    z = _gelu(_dot(h, win_ref[...]))
    u = z[:, :GMLP_DIM]
    v = _layer_norm(z[:, GMLP_DIM:], lng_ref[...], lnb_ref[...])
    if v_ref is not None:
        v_ref[...] = v.reshape(v_ref.shape)
    vb = v.astype(BF16)
    row_id = lax.broadcasted_iota(jnp.int32, (GMLP_CHUNK, GMLP_CHUNK), 0)
    col_id = lax.broadcasted_iota(jnp.int32, (GMLP_CHUNK, GMLP_CHUNK), 1)
    causal = row_id >= col_id
    for gi in range(GMLP_GROUPS):
        w = jnp.where(causal, wmix_ref[gi], jnp.zeros((), BF16))
        bias = bmix_ref[:, gi:gi + 1]
        cols = slice(gi * GMLP_GROUP_WIDTH, (gi + 1) * GMLP_GROUP_WIDTH)
        for c in range(rows // GMLP_CHUNK):
            rws = slice(c * GMLP_CHUNK, (c + 1) * GMLP_CHUNK)
            mixed = _dot(w, vb[rws, cols]) + bias
            um_scr[rws, cols] = (u[rws, cols] * mixed).astype(BF16)
    y = _dot(um_scr[...], wout_ref[...])
    o_ref[...] = _residual_norm(x, y, _mod(mod_ref, 2), g_ref, b_ref)


def _gmlp_call(x, mod, layer, j, w_in, ln_g, ln_b, wmix, bmix, w_out, mix_g, mix_b, tile_rows, emit_v):
    s = _Stream(x, tile_rows)
    out_specs = [s.x_spec()]
    out_shape = [s.x_shape()]
    if emit_v:
        out_specs.append(s.x_spec(GMLP_DIM))
        out_shape.append(s.x_shape(GMLP_DIM))
    res = pl.pallas_call(
        _gmlp_kernel,
        grid=(s.n_tiles,),
        in_specs=[s.x_spec(), s.mod_spec(layer), _layer_spec(w_in, j), _layer_spec(ln_g, j),
                  _layer_spec(ln_b, j), _const_spec(wmix.shape), _const_spec(bmix.shape),
                  _layer_spec(w_out, j), _layer_spec(mix_g, layer), _layer_spec(mix_b, layer)],
        out_specs=out_specs,
        out_shape=out_shape,
        scratch_shapes=[pltpu.VMEM((s.rows, GMLP_DIM), BF16)],
        compiler_params=_params("arbitrary"),
        name=f"gmlp_l{layer}_g{s.g}",
    )(x, mod, w_in, ln_g, ln_b, wmix, bmix, w_out, mix_g, mix_b)
    return res if emit_v else (res[0], None)


CONV_HALO = 32
CONV_ROW_CHUNK = 128
CONV_LANE_CHUNK = 256


def _glu_rows(h, wpw1_ref, bpw1_ref):
    a = _dot(h, wpw1_ref[...]) + bpw1_ref[...]
    return a[:, :D_MODEL] * jax.nn.sigmoid(a[:, D_MODEL:])


def _conv_tail(x, y, mod_ref, cg_ref, cb_ref, wpw2_ref, bpw2_ref, g_ref, b_ref):
    y = _layer_norm(y, cg_ref[...], cb_ref[...])
    y = (y * jax.nn.sigmoid(y)).astype(BF16)
    out = _dot(y, wpw2_ref[...]) + bpw2_ref[...]
    return _residual_norm(x, out, _mod(mod_ref, 2), g_ref, b_ref)


def _conv_p_kernel(x_ref, mod_ref, wpw1_ref, bpw1_ref, wdw_ref, bdw_ref, cg_ref, cb_ref, wpw2_ref, bpw2_ref,
                   g_ref, b_ref, o_ref, st_ref, buf_scr, y_scr):
    t = pl.program_id(1)
    x = x_ref[...]
    rows = x.shape[1]
    off = CONV_HALO - CONV_CTX

    @pl.when(t == 0)
    def _():
        buf_scr[0:CONV_HALO, :] = jnp.zeros((CONV_HALO, D_MODEL), F32)

    h = _modulated_rows(x, mod_ref, 0, 1)
    buf_scr[CONV_HALO:CONV_HALO + rows, :] = _glu_rows(h, wpw1_ref, bpw1_ref)

    n_q = (CONV_WIDTH + SUBLANES - 1) // SUBLANES
    window = CONV_ROW_CHUNK + CONV_HALO
    tiles = CONV_ROW_CHUNK // SUBLANES
    for lc in range(D_MODEL // CONV_LANE_CHUNK):
        lanes = slice(lc * CONV_LANE_CHUNK, (lc + 1) * CONV_LANE_CHUNK)
        for rc in range(rows // CONV_ROW_CHUNK):
            r0 = rc * CONV_ROW_CHUNK
            win = buf_scr[r0:r0 + window, lanes]
            acc = jnp.broadcast_to(bdw_ref[:, lanes][None], (tiles, SUBLANES, CONV_LANE_CHUNK))
            for r in range(SUBLANES):
                shift = off + r
                phase = win if shift % window == 0 else pltpu.roll(win, window - shift, 0)
                for q in range(n_q):
                    k = SUBLANES * q + r
                    if k < CONV_WIDTH:
                        assert SUBLANES * q + CONV_ROW_CHUNK + shift <= window
                        taps = phase[SUBLANES * q:SUBLANES * q + CONV_ROW_CHUNK, :]
                        acc = acc + wdw_ref[k, :, lanes][None] * taps.reshape(tiles, SUBLANES, CONV_LANE_CHUNK)
            y_scr[r0:r0 + CONV_ROW_CHUNK, lanes] = acc.reshape(CONV_ROW_CHUNK, CONV_LANE_CHUNK)

    @pl.when(t == pl.num_programs(1) - 1)
    def _():
        st_ref[...] = buf_scr[rows + off:rows + CONV_HALO, :].reshape(st_ref.shape)

    buf_scr[0:CONV_HALO, :] = buf_scr[rows:rows + CONV_HALO, :]
    o_ref[...] = _conv_tail(x, y_scr[...], mod_ref, cg_ref, cb_ref, wpw2_ref, bpw2_ref, g_ref, b_ref)


def _conv_p_call(x, mod, layer, j, w_pw1, b_pw1, w_dw, b_dw, cln_g, cln_b, w_pw2, b_pw2, mix_g, mix_b,
                 tile_rows):
    n_b, seq, _ = x.shape
    n_t = seq // tile_rows
    xspec = pl.BlockSpec((1, tile_rows, D_MODEL), lambda b, t: (b, t, 0))
    return pl.pallas_call(
        _conv_p_kernel,
        grid=(n_b, n_t),
        in_specs=[xspec,
                  pl.BlockSpec((None, 1, 1, 6 * D_MODEL), lambda b, t: (layer, b, 0, 0)),
                  _layer_spec(w_pw1, j), _layer_spec(b_pw1, j), _layer_spec(w_dw, j), _layer_spec(b_dw, j),
                  _layer_spec(cln_g, j), _layer_spec(cln_b, j), _layer_spec(w_pw2, j), _layer_spec(b_pw2, j),
                  _layer_spec(mix_g, layer), _layer_spec(mix_b, layer)],
        out_specs=[xspec, pl.BlockSpec((1, CONV_CTX, D_MODEL), lambda b, t: (b, 0, 0))],
        out_shape=[jax.ShapeDtypeStruct(x.shape, F32),
                   jax.ShapeDtypeStruct((n_b, CONV_CTX, D_MODEL), F32)],
        scratch_shapes=[pltpu.VMEM((tile_rows + CONV_HALO, D_MODEL), F32),
                        pltpu.VMEM((tile_rows, D_MODEL), F32)],
        compiler_params=_params("arbitrary", "arbitrary"),
        name=f"conv_p_l{layer}",
    )(x, mod, w_pw1, b_pw1, w_dw, b_dw, cln_g, cln_b, w_pw2, b_pw2, mix_g, mix_b)


def _conv_s_kernel(x_ref, mod_ref, st_ref, wpw1_ref, bpw1_ref, coef_ref, bdw_ref, cg_ref, cb_ref,
                   wpw2_ref, bpw2_ref, g_ref, b_ref, o_ref, a_ref):
    x = x_ref[...]
    g, r, _ = x.shape
    h = _modulated_rows(x, mod_ref, 0, 1)
    a = _glu_rows(h, wpw1_ref, bpw1_ref).reshape(g, r, D_MODEL)
    a_ref[...] = a
    y = jnp.broadcast_to(bdw_ref[...], (g, r, D_MODEL))
    for i in range(CONV_CTX):
        y = y + coef_ref[i] * st_ref[:, i:i + 1, :]
    for i in range(r):
        y = y + coef_ref[CONV_CTX + i] * a[:, i:i + 1, :]
    o_ref[...] = _conv_tail(x, y.reshape(g * r, D_MODEL), mod_ref, cg_ref, cb_ref, wpw2_ref, bpw2_ref,
                            g_ref, b_ref)


def _conv_s_call(x, mod, layer, j, state, w_pw1, b_pw1, coef, b_dw, cln_g, cln_b, w_pw2, b_pw2, mix_g, mix_b,
                 tile_rows):
    s = _Stream(x, tile_rows)
    st_spec = pl.BlockSpec((None, s.g, CONV_CTX, D_MODEL), lambda i: (j, i, 0, 0))
    return pl.pallas_call(
        _conv_s_kernel,
        grid=(s.n_tiles,),
        in_specs=[s.x_spec(), s.mod_spec(layer), st_spec, _layer_spec(w_pw1, j), _layer_spec(b_pw1, j),
                  _const_spec(coef.shape), _layer_spec(b_dw, j), _layer_spec(cln_g, j), _layer_spec(cln_b, j),
                  _layer_spec(w_pw2, j), _layer_spec(b_pw2, j), _layer_spec(mix_g, layer),
                  _layer_spec(mix_b, layer)],
        out_specs=[s.x_spec(), s.x_spec()],
        out_shape=[s.x_shape(), s.x_shape()],
        compiler_params=_params("arbitrary"),
        name=f"conv_s_l{layer}",
    )(x, mod, state, w_pw1, b_pw1, coef, b_dw, cln_g, cln_b, w_pw2, b_pw2, mix_g, mix_b)


def _rope_tile(x, cos, sin_signed):
    lane = lax.broadcasted_iota(jnp.int32, x.shape, 1)
    upper = pltpu.roll(x, LANES - MLA_ROPE // 2, 1)
    lower = pltpu.roll(x, MLA_ROPE // 2, 1)
    swapped = jnp.where((lane % MLA_ROPE) < MLA_ROPE // 2, upper, lower)
    return x * cos + swapped * sin_signed


MLA_QK = MLA_KV_RANK + MLA_ROPE


def _mla_proj_kernel(merged, x_ref, mod_ref, wdown_ref, gq_ref, gkv_ref, wuq_ref, wuk_ref, cos_ref, sin_ref,
                     *out_refs):
    if merged:
        q_ref, ckv_ref, kr_ref, kcat_ref = out_refs
        qa_dst = lambda hd: q_ref.at[hd, :, :MLA_KV_RANK]
        qr_dst = lambda hd: q_ref.at[hd, :, MLA_KV_RANK:]
        q_dtype = q_ref.dtype
    else:
        qa_ref, qr_ref, ckv_ref, kr_ref = out_refs
        qa_dst = lambda hd: qa_ref.at[hd]
        qr_dst = lambda hd: qr_ref.at[hd]
        q_dtype = qa_ref.dtype
    x = x_ref[...]
    h = _modulated_rows(x, mod_ref, 0, 1)
    d = _dot(h, wdown_ref[...])
    q_lat = _rms_norm(d[:, :MLA_Q_RANK], gq_ref[...])
    ckv = _rms_norm(d[:, MLA_Q_RANK:MLA_Q_RANK + MLA_KV_RANK], gkv_ref[...])
    cos = cos_ref[...]
    sin = sin_ref[...]
    kr = _rope_tile(d[:, MLA_Q_RANK + MLA_KV_RANK:], cos, sin)[:, :MLA_ROPE]
    ckv_ref[...] = ckv
    kr_ref[...] = kr
    if merged:
        kcat_ref[:, :MLA_KV_RANK] = ckv.astype(BF16)
        kcat_ref[:, MLA_KV_RANK:] = kr.astype(BF16)
    q = _dot(q_lat.astype(BF16), wuq_ref[...])
    n_nope = MLA_HEADS * MLA_NOPE
    for hd in range(MLA_HEADS):
        qn = q[:, hd * MLA_NOPE:(hd + 1) * MLA_NOPE].astype(BF16)
        qa_dst(hd)[...] = (_dot(qn, wuk_ref[hd]) * MLA_SCALE).astype(q_dtype)
    for t in range(MLA_HEADS * MLA_ROPE // LANES):
        rot = _rope_tile(q[:, n_nope + t * LANES:n_nope + (t + 1) * LANES], cos, sin) * MLA_SCALE
        qr_dst(2 * t)[...] = rot[:, :MLA_ROPE].astype(q_dtype)
        qr_dst(2 * t + 1)[...] = rot[:, MLA_ROPE:].astype(q_dtype)


def _mla_proj_call(x, mod, layer, j, w_down, g_q, g_kv, w_uq, w_uk, cos, sin, tile_rows, merged, pos_tiles):
    s = _Stream(x, tile_rows)
    n = s.n_tokens
    if pos_tiles == 1:
        pos_spec = pl.BlockSpec((s.rows, LANES), lambda i: (0, 0))
    else:
        pos_spec = pl.BlockSpec((s.rows, LANES), lambda i: (i % pos_tiles, 0))
    if merged:
        out_specs = [pl.BlockSpec((MLA_HEADS, s.rows, MLA_QK), lambda i: (0, i, 0))]
        out_shape = [jax.ShapeDtypeStruct((MLA_HEADS, n, MLA_QK), BF16)]
    else:
        out_specs = [pl.BlockSpec((MLA_HEADS, s.rows, MLA_KV_RANK), lambda i: (0, i, 0)),
                     pl.BlockSpec((MLA_HEADS, s.rows, MLA_ROPE), lambda i: (0, i, 0))]
        out_shape = [jax.ShapeDtypeStruct((MLA_HEADS, n, MLA_KV_RANK), F32),
                     jax.ShapeDtypeStruct((MLA_HEADS, n, MLA_ROPE), F32)]
    out_specs += [s.rows_spec(MLA_KV_RANK), s.rows_spec(MLA_ROPE)]
    out_shape += [jax.ShapeDtypeStruct((n, MLA_KV_RANK), F32), jax.ShapeDtypeStruct((n, MLA_ROPE), F32)]
    if merged:
        out_specs.append(s.rows_spec(MLA_QK))
        out_shape.append(jax.ShapeDtypeStruct((n, MLA_QK), BF16))
    return pl.pallas_call(
        functools.partial(_mla_proj_kernel, merged),
        grid=(s.n_tiles,),
        in_specs=[s.x_spec(), s.mod_spec(layer), _layer_spec(w_down, j), _layer_spec(g_q, j),
                  _layer_spec(g_kv, j), _layer_spec(w_uq, j), _layer_spec(w_uk, j), pos_spec, pos_spec],
        out_specs=out_specs,
        out_shape=out_shape,
        compiler_params=_params("arbitrary"),
        name=f"mla_proj_l{layer}_g{s.g}",
    )(x, mod, w_down, g_q, g_kv, w_uq, w_uk, cos, sin)


ATTN_Q_TILE = 512
ATTN_K_TILE = 512
ATTN_ROW_PARTS = 8


def _attn_p_kernel(q_ref, k_ref, o_ref, m_scr, l_scr, acc_scr, s_scr):
    i = pl.program_id(1)
    heads, tq, _ = q_ref.shape
    rows = heads * tq
    tk = ATTN_K_TILE
    m_scr[...] = jnp.full(m_scr.shape, NEG_BIG, F32)
    l_scr[...] = jnp.zeros(l_scr.shape, F32)
    acc_scr[...] = jnp.zeros(acc_scr.shape, F32)
    part_rows = rows // ATTN_ROW_PARTS
    parts = [slice(p * part_rows, (p + 1) * part_rows) for p in range(ATTN_ROW_PARTS)]

    def q_part(rs):
        return q_ref[rs.start // tq:rs.stop // tq].reshape(part_rows, MLA_QK)

    def keys(jb):
        return k_ref[pl.ds(pl.multiple_of(jb * tk, tk), tk), :]

    def absorb(rs, s, vb, masked):
        if masked:
            q_pos = lax.broadcasted_iota(jnp.int32, s.shape, 0) % tq
            k_pos = lax.broadcasted_iota(jnp.int32, s.shape, 1)
            s = jnp.where(k_pos <= q_pos, s, NEG_BIG)
        m_prev = m_scr[rs, :]
        m_next = jnp.maximum(m_prev, jnp.max(s, axis=1, keepdims=True))
        p = jnp.exp(s - jnp.concatenate([m_next] * (tk // LANES), axis=1))
        alpha = jnp.exp(m_prev - m_next)
        l_scr[rs, :] = alpha * l_scr[rs, :] + jnp.sum(p, axis=1, keepdims=True)
        m_scr[rs, :] = m_next
        acc_scr[rs, :] = (acc_scr[rs, :] * jnp.concatenate([alpha] * (MLA_KV_RANK // LANES), axis=1)
                          + _dot(p.astype(BF16), vb))

    kb0 = keys(0)
    for rs in parts:
        s_scr[rs, :] = _dot_nt(q_part(rs), kb0)

    def body(jb, carry):
        vb = keys(jb)[:, :MLA_KV_RANK]
        kb_next = keys(jb + 1)
        for rs in parts:
            s = s_scr[rs, :]
            s_scr[rs, :] = _dot_nt(q_part(rs), kb_next)
            absorb(rs, s, vb, False)
        return carry

    lax.fori_loop(0, i, body, 0)
    vb = keys(i)[:, :MLA_KV_RANK]
    for rs in parts:
        absorb(rs, s_scr[rs, :], vb, True)
    inv = 1.0 / l_scr[...]
    ctx = acc_scr[...] * jnp.concatenate([inv] * (MLA_KV_RANK // LANES), axis=1)
    for hd in range(heads):
        o_ref[:, hd * MLA_KV_RANK:(hd + 1) * MLA_KV_RANK] = ctx[hd * tq:(hd + 1) * tq].astype(o_ref.dtype)


def _attn_p_call(q, kcat, n_batch, seq):
    assert ATTN_Q_TILE == ATTN_K_TILE
    nq = seq // ATTN_Q_TILE
    rows = MLA_HEADS * ATTN_Q_TILE
    return pl.pallas_call(
        _attn_p_kernel,
        grid=(n_batch, nq),
        in_specs=[pl.BlockSpec((MLA_HEADS, ATTN_Q_TILE, MLA_QK), lambda b, i: (0, b * nq + i, 0)),
                  pl.BlockSpec((seq, MLA_QK), lambda b, i: (b, 0))],
        out_specs=pl.BlockSpec((ATTN_Q_TILE, MLA_HEADS * MLA_KV_RANK), lambda b, i: (b * nq + i, 0)),
        out_shape=jax.ShapeDtypeStruct((n_batch * seq, MLA_HEADS * MLA_KV_RANK), BF16),
        scratch_shapes=[pltpu.VMEM((rows, LANES), F32), pltpu.VMEM((rows, LANES), F32),
                        pltpu.VMEM((rows, MLA_KV_RANK), F32), pltpu.VMEM((rows, ATTN_K_TILE), F32)],
        compiler_params=_params("arbitrary", "arbitrary"),
        name="mla_attn_p",
    )(q, kcat)


ATTN_S_SPLITS = 4


def _attn_s_kernel(j, pt_ref, qa_ref, qr_ref, kn_ref, rn_ref, cache_k, cache_rt, o_ref, kbuf, rbuf, sems):
    b = pl.program_id(0)
    nb = pl.num_programs(0)
    n_pages = pt_ref.shape[1]
    page = cache_k.shape[2]
    slot = b % 2

    def copies(pg, sl, p):
        keys = pl.ds(p * page, page)
        return (pltpu.make_async_copy(cache_k.at[j, pg], kbuf.at[sl, keys, :], sems.at[0, sl]),
                pltpu.make_async_copy(cache_rt.at[j, pg], rbuf.at[sl, :, keys], sems.at[1, sl]))

    def start_all(bi, sl):
        for p in range(n_pages):
            for c in copies(pt_ref[bi, p], sl, p):
                c.start()

    def wait_all(sl):
        for p in range(n_pages):
            for c in copies(0, sl, p):
                c.wait()

    @pl.when(b == 0)
    def _():
        start_all(0, 0)

    @pl.when(b + 1 < nb)
    def _():
        start_all(b + 1, 1 - slot)

    wait_all(slot)

    heads, t, _ = qa_ref.shape
    rows = heads * t
    qa = qa_ref[...].reshape(rows, MLA_KV_RANK).astype(BF16)
    qr = qr_ref[...].reshape(rows, MLA_ROPE).astype(BF16)
    kn = kn_ref[...].astype(BF16)
    rn = rn_ref[...].astype(BF16)

    s_new = _dot_nt(qa, kn) + _dot_nt(qr, rn)
    q_pos = lax.broadcasted_iota(jnp.int32, s_new.shape, 0) % t
    k_pos = lax.broadcasted_iota(jnp.int32, s_new.shape, 1)
    s_new = jnp.where(k_pos <= q_pos, s_new, NEG_BIG)
    m_new = jnp.max(s_new, axis=1, keepdims=True)
    p_new = jnp.exp(s_new - m_new)
    parts = [(m_new, jnp.sum(p_new, axis=1, keepdims=True), _dot(p_new.astype(BF16), kn))]

    span = n_pages * page // ATTN_S_SPLITS
    for c in range(ATTN_S_SPLITS):
        keys = slice(c * span, (c + 1) * span)
        kp = kbuf[slot, keys, :].astype(BF16)
        rp = rbuf[slot, :, keys].astype(BF16)
        s = _dot_nt(qa, kp) + _dot(qr, rp)
        m_c = jnp.max(s, axis=1, keepdims=True)
        p = jnp.exp(s - m_c)
        parts.append((m_c, jnp.sum(p, axis=1, keepdims=True), _dot(p.astype(BF16), kp)))

    m = functools.reduce(jnp.maximum, [pt[0] for pt in parts])
    scales = [jnp.exp(pt[0] - m) for pt in parts]
    denom = sum(sc * pt[1] for sc, pt in zip(scales, parts))
    ctx = sum(sc * pt[2] for sc, pt in zip(scales, parts)) / denom
    for hd in range(heads):
        o_ref[:, hd * MLA_KV_RANK:(hd + 1) * MLA_KV_RANK] = ctx[hd * t:(hd + 1) * t]


def _attn_s_call(page_table, qa, qr, ckv_new, kr_new, cache_k, cache_r, j, n_batch, t):
    past = page_table.shape[1] * cache_k.shape[2]
    grid_spec = pltpu.PrefetchScalarGridSpec(
        num_scalar_prefetch=1,
        grid=(n_batch,),
        in_specs=[pl.BlockSpec((MLA_HEADS, t, MLA_KV_RANK), lambda b, pt: (0, b, 0)),
                  pl.BlockSpec((MLA_HEADS, t, MLA_ROPE), lambda b, pt: (0, b, 0)),
                  pl.BlockSpec((t, MLA_KV_RANK), lambda b, pt: (b, 0)),
                  pl.BlockSpec((t, MLA_ROPE), lambda b, pt: (b, 0)),
                  pl.BlockSpec(memory_space=pl.ANY),
                  pl.BlockSpec(memory_space=pl.ANY)],
        out_specs=pl.BlockSpec((t, MLA_HEADS * MLA_KV_RANK), lambda b, pt: (b, 0)),
        scratch_shapes=[pltpu.VMEM((2, past, MLA_KV_RANK), F32),
                        pltpu.VMEM((2, MLA_ROPE, past), F32),
                        pltpu.SemaphoreType.DMA((2, 2))],
    )
    return pl.pallas_call(
        functools.partial(_attn_s_kernel, j),
        grid_spec=grid_spec,
        out_shape=jax.ShapeDtypeStruct((n_batch * t, MLA_HEADS * MLA_KV_RANK), F32),
        compiler_params=_params("arbitrary"),
        name="mla_attn_s",
    )(page_table, qa, qr, ckv_new, kr_new, cache_k, cache_r)


def _mla_out_kernel(x_ref, mod_ref, ctx_ref, wuv_ref, wo_ref, g_ref, b_ref, o_ref):
    x = x_ref[...]
    parts = []
    for hd in range(MLA_HEADS):
        c = ctx_ref[:, hd * MLA_KV_RANK:(hd + 1) * MLA_KV_RANK].astype(BF16)
        parts.append(_dot(c, wuv_ref[hd]).astype(BF16))
    o = jnp.concatenate(parts, axis=1)
    y = _dot(o, wo_ref[...])
    o_ref[...] = _residual_norm(x, y, _mod(mod_ref, 2), g_ref, b_ref)


def _mla_out_call(x, mod, layer, j, ctx, w_uv, w_o, mix_g, mix_b, tile_rows):
    s = _Stream(x, tile_rows)
    return pl.pallas_call(
        _mla_out_kernel,
        grid=(s.n_tiles,),
        in_specs=[s.x_spec(), s.mod_spec(layer), s.rows_spec(MLA_HEADS * MLA_KV_RANK),
                  _layer_spec(w_uv, j), _layer_spec(w_o, j), _layer_spec(mix_g, layer),
                  _layer_spec(mix_b, layer)],
        out_specs=s.x_spec(),
        out_shape=s.x_shape(),
        compiler_params=_params("arbitrary"),
        name=f"mla_out_l{layer}_g{s.g}",
    )(x, mod, ctx, w_uv, w_o, mix_g, mix_b)


def _row3(a):
    return a.reshape(a.shape[0], 1, a.shape[1])


def _rope_tables(pos):
    half = MLA_ROPE // 2
    inv = ROPE_THETA ** (-jnp.arange(half, dtype=F32) / half)
    ang = pos.astype(F32)[:, None] * inv[None, :]
    cos, sin = jnp.cos(ang), jnp.sin(ang)
    cos64 = jnp.concatenate([cos, cos], -1)
    sin64 = jnp.concatenate([-sin, sin], -1)
    return jnp.tile(cos64, (1, LANES // MLA_ROPE)), jnp.tile(sin64, (1, LANES // MLA_ROPE))


def kernel(x_prompt, x_sample, cache_ckv, cache_krope, state_conv, page_table, c_prompt, c_sample,
           ada_w, ada_b, ln_mix_g, ln_mix_b, ln_ffn_g, ln_ffn_b, ffn_w_gate, ffn_w_up, ffn_w_down,
           gmlp_w_in, gmlp_ln_g, gmlp_ln_b, gmlp_w_s, gmlp_b_s, gmlp_w_out,
           conv_w_pw1, conv_b_pw1, conv_w_dw, conv_b_dw, conv_ln_g, conv_ln_b, conv_w_pw2, conv_b_pw2,
           mla_w_down, mla_g_q, mla_g_kv, mla_w_uq, mla_w_uk, mla_w_uv, mla_w_o):
    n_b, seq, _ = x_prompt.shape
    n_db, dec_seq, _ = x_sample.shape
    past_len = page_table.shape[1] * cache_ckv.shape[2]

    c_p = jnp.pad(c_prompt, ((0, SUBLANES - n_b), (0, 0)))
    mod_p, mod_s = _ada_call(c_p, c_sample, ada_w, ada_b)
    mod_p = mod_p.reshape(DEPTH, SUBLANES, 1, 6 * D_MODEL)
    mod_s = mod_s.reshape(DEPTH, n_db, 1, 6 * D_MODEL)

    ln_mix_g, ln_mix_b, ln_ffn_g, ln_ffn_b = map(_row3, (ln_mix_g, ln_mix_b, ln_ffn_g, ln_ffn_b))
    wg, wu, wd = (w.astype(BF16) for w in (ffn_w_gate, ffn_w_up, ffn_w_down))

    g_w_in, g_w_out = gmlp_w_in.astype(BF16), gmlp_w_out.astype(BF16)
    g_ln_g, g_ln_b = _row3(gmlp_ln_g), _row3(gmlp_ln_b)
    reps = GMLP_CHUNK // dec_seq
    eye = jnp.eye(reps, dtype=F32)

    c_w_pw1, c_w_pw2 = conv_w_pw1.astype(BF16), conv_w_pw2.astype(BF16)
    c_b_pw1, c_b_dw, c_ln_g, c_ln_b, c_b_pw2 = map(_row3, (conv_b_pw1, conv_b_dw, conv_ln_g, conv_ln_b,
                                                            conv_b_pw2))
    c_w_dw8 = jnp.broadcast_to(conv_w_dw[:, :, None, :], conv_w_dw.shape[:2] + (SUBLANES, D_MODEL))
    tap = jnp.arange(CONV_CTX + dec_seq)[:, None] - jnp.arange(dec_seq)[None, :]
    tap_ok = (tap >= 0) & (tap < CONV_WIDTH)
    conv_coef = jnp.where(tap_ok[None, :, :, None], conv_w_dw[:, jnp.clip(tap, 0, CONV_WIDTH - 1)], 0.0)

    n_mla = mla_w_down.shape[0]
    m_w_down = jnp.pad(mla_w_down, ((0, 0), (0, 0), (0, MLA_DOWN_PAD - mla_w_down.shape[-1]))).astype(BF16)
    w_uq = mla_w_uq.reshape(n_mla, MLA_Q_RANK, MLA_HEADS, MLA_NOPE + MLA_ROPE)
    m_w_uq = jnp.concatenate([w_uq[..., :MLA_NOPE].reshape(n_mla, MLA_Q_RANK, -1),
                              w_uq[..., MLA_NOPE:].reshape(n_mla, MLA_Q_RANK, -1)], -1).astype(BF16)
    m_w_uk = mla_w_uk.transpose(0, 2, 3, 1).astype(BF16)
    m_w_uv = mla_w_uv.transpose(0, 2, 1, 3).astype(BF16)
    m_w_o = mla_w_o.astype(BF16)
    m_g_q, m_g_kv = _row3(mla_g_q), _row3(mla_g_kv)
    cos_p, sin_p = _rope_tables(jnp.arange(seq))
    cos_s, sin_s = _rope_tables(past_len + jnp.arange(dec_seq))

    tile_p, tile_s = 512, 512
    gm_tile = 512
    conv_tile = 256
    mla_tile = 512
    s_reps = mla_tile // dec_seq
    cos_s, sin_s = jnp.tile(cos_s, (s_reps, 1)), jnp.tile(sin_s, (s_reps, 1))

    xp, xs = x_prompt, x_sample
    ckv_p_rows, kr_p_rows, ckv_s_rows, kr_s_rows = [], [], [], []
    conv_p_states, conv_s_states, gmlp_v_rows = [], [], []
    for i in range(DEPTH):
        kind, j = i % N_MIXERS, i // N_MIXERS
        if kind == 0:
            wmix_p = gmlp_w_s[j].astype(BF16)
            bmix_p = gmlp_b_s[j].T
            corner = gmlp_w_s[j, :, :dec_seq, :dec_seq]
            wmix_s = jnp.einsum('ab,gts->gatbs', eye, corner).reshape(
                GMLP_GROUPS, GMLP_CHUNK, GMLP_CHUNK).astype(BF16)
            bmix_s = jnp.tile(gmlp_b_s[j, :, :dec_seq].T, (reps, 1))
            xp, _ = _gmlp_call(xp, mod_p, i, j, g_w_in, g_ln_g, g_ln_b, wmix_p, bmix_p, g_w_out,
                               ln_mix_g, ln_mix_b, gm_tile, False)
            xs, v_s = _gmlp_call(xs, mod_s, i, j, g_w_in, g_ln_g, g_ln_b, wmix_s, bmix_s, g_w_out,
                                 ln_mix_g, ln_mix_b, gm_tile, True)
            gmlp_v_rows.append(v_s)
        elif kind == 1:
            xp, st_p = _conv_p_call(xp, mod_p, i, j, c_w_pw1, c_b_pw1, c_w_dw8, c_b_dw, c_ln_g, c_ln_b,
                                    c_w_pw2, c_b_pw2, ln_mix_g, ln_mix_b, conv_tile)
            xs_new, a_s = _conv_s_call(xs, mod_s, i, j, state_conv, c_w_pw1, c_b_pw1, conv_coef[j], c_b_dw,
                                       c_ln_g, c_ln_b, c_w_pw2, c_b_pw2, ln_mix_g, ln_mix_b, conv_tile)
            xs = xs_new
            conv_p_states.append(st_p)
            conv_s_states.append(jnp.concatenate([state_conv[j][:, dec_seq:], a_s], axis=1))
        else:
            q_p, ckv, kr, kcat = _mla_proj_call(
                xp, mod_p, i, j, m_w_down, m_g_q, m_g_kv, m_w_uq, m_w_uk, cos_p, sin_p, mla_tile, True,
                seq // mla_tile)
            ctx_p = _attn_p_call(q_p, kcat, n_b, seq)
            xp = _mla_out_call(xp, mod_p, i, j, ctx_p, m_w_uv, m_w_o, ln_mix_g, ln_mix_b, mla_tile)
            ckv_p_rows.append(ckv.reshape(n_b, seq, MLA_KV_RANK))
            kr_p_rows.append(kr.reshape(n_b, seq, MLA_ROPE))
            qa, qr, ckv, kr = _mla_proj_call(
                xs, mod_s, i, j, m_w_down, m_g_q, m_g_kv, m_w_uq, m_w_uk, cos_s, sin_s, mla_tile, False, 1)
            ctx_s = _attn_s_call(page_table, qa, qr, ckv, kr, cache_ckv, jnp.swapaxes(cache_krope, 2, 3), j,
                                 n_db, dec_seq)
            xs = _mla_out_call(xs, mod_s, i, j, ctx_s, m_w_uv, m_w_o, ln_mix_g, ln_mix_b, mla_tile)
            ckv_s_rows.append(ckv.reshape(n_db, dec_seq, MLA_KV_RANK))
            kr_s_rows.append(kr.reshape(n_db, dec_seq, MLA_ROPE))
        xp = _ffn_call(xp, mod_p, i, wg, wu, wd, ln_ffn_g, ln_ffn_b, tile_p)
        xs = _ffn_call(xs, mod_s, i, wg, wu, wd, ln_ffn_g, ln_ffn_b, tile_s)
    return (xp, xs, jnp.stack(ckv_p_rows), jnp.stack(kr_p_rows), jnp.stack(ckv_s_rows), jnp.stack(kr_s_rows),
            jnp.stack(conv_p_states), jnp.stack(conv_s_states), jnp.stack(gmlp_v_rows))
```

```python
import functools
import math

import jax
import jax.numpy as jnp
from jax import lax
from jax.experimental import pallas as pl
from jax.experimental.pallas import tpu as pltpu

F32 = jnp.float32
BF16 = jnp.bfloat16

D_MODEL = 1024
DEPTH = 4
N_MIXERS = 3
ALPHA = (2.0 * DEPTH) ** 0.25
LN_EPS = 1e-5
RMS_EPS = 1e-6

GMLP_DIM = 2 * D_MODEL
GMLP_GROUPS = 8
GMLP_CHUNK = 128
GMLP_GROUP_WIDTH = GMLP_DIM // GMLP_GROUPS

CONV_WIDTH = 31
CONV_CTX = CONV_WIDTH - 1

MLA_HEADS = 8
MLA_Q_RANK = 384
MLA_KV_RANK = 256
MLA_NOPE = 128
MLA_ROPE = 64
MLA_V = 128
ROPE_THETA = 10000.0
MLA_SCALE = (MLA_NOPE + MLA_ROPE) ** -0.5
MLA_DOWN_PAD = MLA_Q_RANK + MLA_KV_RANK + 128

FFN_DIM = ((8 * D_MODEL + 3 * 256 - 1) // (3 * 256)) * 256

VMEM_LIMIT_BYTES = 56 * 1024 * 1024
LANES = 128
SUBLANES = 8

NEG_BIG = -1e30


def _params(*semantics):
    return pltpu.CompilerParams(dimension_semantics=semantics, vmem_limit_bytes=VMEM_LIMIT_BYTES)


def _dot(a, b):
    return jnp.dot(a, b, preferred_element_type=F32)


def _dot_nt(a, b):
    return lax.dot_general(a, b, (((1,), (1,)), ((), ())), preferred_element_type=F32)


def _layer_norm(x, g, b):
    mu = jnp.mean(x, -1, keepdims=True)
    xc = x - mu
    var = jnp.mean(xc * xc, -1, keepdims=True)
    return xc * lax.rsqrt(var + LN_EPS) * g + b


def _rms_norm(x, g):
    return x * lax.rsqrt(jnp.mean(x * x, -1, keepdims=True) + RMS_EPS) * g


def _mod(mod_ref, k, groups=slice(None)):
    return mod_ref[groups, :, k * D_MODEL:(k + 1) * D_MODEL]


def _modulated_rows(x, mod_ref, k_shift, k_scale, groups=slice(None)):
    g, r, d = x.shape
    h = x * (1.0 + _mod(mod_ref, k_scale, groups)) + _mod(mod_ref, k_shift, groups)
    return h.reshape(g * r, d).astype(BF16)


def _row_parts(x_ref, n):
    g, r, _ = x_ref.shape
    if g == 1:
        return [(slice(0, 1), slice(p * (r // n), (p + 1) * (r // n))) for p in range(n)]
    return [(slice(p * (g // n), (p + 1) * (g // n)), slice(0, r)) for p in range(n)]


def _residual_norm(x, y_rows, gate, g_ref, b_ref):
    return _layer_norm(ALPHA * x + gate * y_rows.reshape(x.shape), g_ref[...], b_ref[...])


def _ada_kernel(cp_ref, cs_ref, w_ref, b_ref, mp_ref, ms_ref):
    w = w_ref[...].astype(BF16)
    bias = b_ref[...]
    for c_ref, o_ref in ((cp_ref, mp_ref), (cs_ref, ms_ref)):
        c = c_ref[...]
        a = (c * jax.nn.sigmoid(c)).astype(BF16)
        o_ref[...] = _dot(a, w) + bias


def _ada_call(c_p, c_s, ada_w, ada_b):
    tn = 1536
    n_p, n_s = c_p.shape[0], c_s.shape[0]
    d6 = ada_w.shape[-1]
    return pl.pallas_call(
        _ada_kernel,
        grid=(DEPTH, d6 // tn),
        in_specs=[
            pl.BlockSpec((n_p, D_MODEL), lambda l, n: (0, 0)),
            pl.BlockSpec((n_s, D_MODEL), lambda l, n: (0, 0)),
            pl.BlockSpec((None, D_MODEL, tn), lambda l, n: (l, 0, n)),
            pl.BlockSpec((None, 1, tn), lambda l, n: (l, 0, n)),
        ],
        out_specs=[
            pl.BlockSpec((None, n_p, tn), lambda l, n: (l, 0, n)),
            pl.BlockSpec((None, n_s, tn), lambda l, n: (l, 0, n)),
        ],
        out_shape=[
            jax.ShapeDtypeStruct((DEPTH, n_p, d6), F32),
            jax.ShapeDtypeStruct((DEPTH, n_s, d6), F32),
        ],
        compiler_params=_params("arbitrary", "arbitrary"),
        name="ada_table",
    )(c_p, c_s, ada_w, ada_b.reshape(DEPTH, 1, d6))


class _Stream:
    def __init__(self, x, tile_rows):
        self.n_groups, self.group_rows, _ = x.shape
        if self.group_rows >= tile_rows:
            assert self.group_rows % tile_rows == 0
            self.g, self.r = 1, tile_rows
            self.tiles_per_group = self.group_rows // tile_rows
        else:
            assert tile_rows % self.group_rows == 0
            self.g, self.r = tile_rows // self.group_rows, self.group_rows
            assert self.n_groups % self.g == 0
            self.tiles_per_group = 1
        self.rows = self.g * self.r
        self.n_tiles = (self.n_groups // self.g) * self.tiles_per_group
        self.n_tokens = self.n_groups * self.group_rows

    def _gi(self, i):
        if self.tiles_per_group == 1:
            return i, 0
        return i // self.tiles_per_group, i % self.tiles_per_group

    def x_spec(self, width=D_MODEL):
        return pl.BlockSpec((self.g, self.r, width), lambda i: (*self._gi(i), 0))

    def mod_spec(self, layer):
        return pl.BlockSpec((None, self.g, 1, 6 * D_MODEL), lambda i: (layer, self._gi(i)[0], 0, 0))

    def rows_spec(self, width):
        return pl.BlockSpec((self.rows, width), lambda i: (i, 0))

    def x_shape(self, width=D_MODEL, dtype=F32):
        return jax.ShapeDtypeStruct((self.n_groups, self.group_rows, width), dtype)


def _const_spec(shape):
    zeros = (0,) * len(shape)
    return pl.BlockSpec(shape, lambda *_: zeros, pipeline_mode=pl.Buffered(1))


def _layer_spec(arr, layer):
    zeros = (0,) * (arr.ndim - 1)
    return pl.BlockSpec((None,) + arr.shape[1:], lambda *_: (layer,) + zeros,
                        pipeline_mode=pl.Buffered(1))


MXU_WIDTH = 256
FFN_SPLIT = (FFN_DIM // MXU_WIDTH + 1) // 2 * MXU_WIDTH


FFN_ROW_PARTS = 2
GMLP_ROW_PARTS = 2
CONV_ROW_PARTS = 2


def _ffn_kernel(x_ref, mod_ref, wg_ref, wu_ref, wd_ref, g_ref, b_ref, o_ref, a_scr):
    parts = _row_parts(x_ref, FFN_ROW_PARTS)
    xs = [x_ref[gs, rs, :] for gs, rs in parts]
    hs = [_modulated_rows(x, mod_ref, 3, 4, gs) for x, (gs, _) in zip(xs, parts)]
    fs = []
    r0 = 0
    for h in hs:
        rows = h.shape[0]
        for sl in (slice(0, FFN_SPLIT), slice(FFN_SPLIT, FFN_DIM)):
            gate = _dot(h, wg_ref[:, sl])
            up = _dot(h, wu_ref[:, sl])
            a_scr[r0:r0 + rows, sl] = (gate * jax.nn.sigmoid(gate) * up).astype(BF16)
        fs.append(_dot(a_scr[r0:r0 + rows, :], wd_ref[...]))
        r0 += rows
    for x, f, (gs, rs) in zip(xs, fs, parts):
        o_ref[gs, rs, :] = _residual_norm(x, f, _mod(mod_ref, 5, gs), g_ref, b_ref)


def _ffn_call(x, mod, layer, wg, wu, wd, ln_g, ln_b, tile_rows):
    s = _Stream(x, tile_rows)
    return pl.pallas_call(
        _ffn_kernel,
        grid=(s.n_tiles,),
        in_specs=[s.x_spec(), s.mod_spec(layer), _layer_spec(wg, layer), _layer_spec(wu, layer),
                  _layer_spec(wd, layer), _layer_spec(ln_g, layer), _layer_spec(ln_b, layer)],
        out_specs=s.x_spec(),
        out_shape=s.x_shape(),
        scratch_shapes=[pltpu.VMEM((s.rows, FFN_DIM), BF16)],
        compiler_params=_params("arbitrary"),
        name=f"ffn_l{layer}_g{s.g}",
    )(x, mod, wg, wu, wd, ln_g, ln_b)


def _gelu(z):
    return 0.5 * z * (1.0 + lax.erf(z * (1.0 / math.sqrt(2.0))))


def _gmlp_kernel(x_ref, mod_ref, win_ref, lng_ref, lnb_ref, wmix_ref, bmix_ref, wout_ref, g_ref, b_ref,
                 o_ref, *rest):
    if len(rest) == 2:
        v_ref, um_scr = rest
    else:
        v_ref, (um_scr,) = None, rest
    row_id = lax.broadcasted_iota(jnp.int32, (GMLP_CHUNK, GMLP_CHUNK), 0)
    col_id = lax.broadcasted_iota(jnp.int32, (GMLP_CHUNK, GMLP_CHUNK), 1)
    causal = row_id >= col_id
    parts = _row_parts(x_ref, GMLP_ROW_PARTS)
    xs = [x_ref[gs, rs, :] for gs, rs in parts]
    zs = [_dot(_modulated_rows(x, mod_ref, 0, 1, gs), win_ref[...]) for x, (gs, _) in zip(xs, parts)]
    ys = []
    r0 = 0
    for x, z, (gs, rs) in zip(xs, zs, parts):
        rows = z.shape[0]
        z = _gelu(z)
        u = z[:, :GMLP_DIM]
        v = _layer_norm(z[:, GMLP_DIM:], lng_ref[...], lnb_ref[...])
        if v_ref is not None:
            v_ref[gs, rs, :] = v.reshape(x.shape[:2] + (GMLP_DIM,))
        vb = v.astype(BF16)
        for gi in range(GMLP_GROUPS):
            w = jnp.where(causal, wmix_ref[gi], jnp.zeros((), BF16))
            bias = bmix_ref[:, gi:gi + 1]
            cols = slice(gi * GMLP_GROUP_WIDTH, (gi + 1) * GMLP_GROUP_WIDTH)
            for c in range(rows // GMLP_CHUNK):
                rws = slice(c * GMLP_CHUNK, (c + 1) * GMLP_CHUNK)
                mixed = _dot(w, vb[rws, cols]) + bias
                um_scr[r0 + c * GMLP_CHUNK:r0 + (c + 1) * GMLP_CHUNK, cols] = (u[rws, cols] * mixed).astype(BF16)
        ys.append(_dot(um_scr[r0:r0 + rows, :], wout_ref[...]))
        r0 += rows
    for x, y, (gs, rs) in zip(xs, ys, parts):
        o_ref[gs, rs, :] = _residual_norm(x, y, _mod(mod_ref, 2, gs), g_ref, b_ref)


def _gmlp_call(x, mod, layer, j, w_in, ln_g, ln_b, wmix, bmix, w_out, mix_g, mix_b, tile_rows, emit_v):
    s = _Stream(x, tile_rows)
    out_specs = [s.x_spec()]
    out_shape = [s.x_shape()]
    if emit_v:
        out_specs.append(s.x_spec(GMLP_DIM))
        out_shape.append(s.x_shape(GMLP_DIM))
    res = pl.pallas_call(
        _gmlp_kernel,
        grid=(s.n_tiles,),
        in_specs=[s.x_spec(), s.mod_spec(layer), _layer_spec(w_in, j), _layer_spec(ln_g, j),
                  _layer_spec(ln_b, j), _const_spec(wmix.shape), _const_spec(bmix.shape),
                  _layer_spec(w_out, j), _layer_spec(mix_g, layer), _layer_spec(mix_b, layer)],
        out_specs=out_specs,
        out_shape=out_shape,
        scratch_shapes=[pltpu.VMEM((s.rows, GMLP_DIM), BF16)],
        compiler_params=_params("arbitrary"),
        name=f"gmlp_l{layer}_g{s.g}",
    )(x, mod, w_in, ln_g, ln_b, wmix, bmix, w_out, mix_g, mix_b)
    return res if emit_v else (res[0], None)


CONV_HALO = 32
CONV_ROW_CHUNK = 128
CONV_LANE_CHUNK = 256


def _glu_rows(h, wpw1_ref, bpw1_ref):
    a = _dot(h, wpw1_ref[...]) + bpw1_ref[...]
    return a[:, :D_MODEL] * jax.nn.sigmoid(a[:, D_MODEL:])


def _conv_tail(x, y, mod_ref, cg_ref, cb_ref, wpw2_ref, bpw2_ref, g_ref, b_ref):
    y = _layer_norm(y, cg_ref[...], cb_ref[...])
    y = (y * jax.nn.sigmoid(y)).astype(BF16)
    out = _dot(y, wpw2_ref[...]) + bpw2_ref[...]
    return _residual_norm(x, out, _mod(mod_ref, 2), g_ref, b_ref)


def _conv_p_kernel(x_ref, mod_ref, wpw1_ref, bpw1_ref, wdw_ref, bdw_ref, cg_ref, cb_ref, wpw2_ref, bpw2_ref,
                   g_ref, b_ref, o_ref, st_ref, buf_scr, y_scr):
    t = pl.program_id(1)
    rows = x_ref.shape[1]
    part_rows = rows // CONV_ROW_PARTS
    off = CONV_HALO - CONV_CTX

    @pl.when(t == 0)
    def _():
        buf_scr[0:CONV_HALO, :] = jnp.zeros((CONV_HALO, D_MODEL), F32)

    xs = [x_ref[:, p * part_rows:(p + 1) * part_rows, :] for p in range(CONV_ROW_PARTS)]
    for p, x in enumerate(xs):
        h = _modulated_rows(x, mod_ref, 0, 1)
        buf_scr[CONV_HALO + p * part_rows:CONV_HALO + (p + 1) * part_rows, :] = _glu_rows(h, wpw1_ref, bpw1_ref)

    n_q = (CONV_WIDTH + SUBLANES - 1) // SUBLANES
    window = CONV_ROW_CHUNK + CONV_HALO
    tiles = CONV_ROW_CHUNK // SUBLANES
    for p, x in enumerate(xs):
        for lc in range(D_MODEL // CONV_LANE_CHUNK):
            lanes = slice(lc * CONV_LANE_CHUNK, (lc + 1) * CONV_LANE_CHUNK)
            for rc in range(part_rows // CONV_ROW_CHUNK):
                r0 = p * part_rows + rc * CONV_ROW_CHUNK
                win = buf_scr[r0:r0 + window, lanes]
                acc = jnp.broadcast_to(bdw_ref[:, lanes][None], (tiles, SUBLANES, CONV_LANE_CHUNK))
                for r in range(SUBLANES):
                    shift = off + r
                    phase = win if shift % window == 0 else pltpu.roll(win, window - shift, 0)
                    for q in range(n_q):
                        k = SUBLANES * q + r
                        if k < CONV_WIDTH:
                            assert SUBLANES * q + CONV_ROW_CHUNK + shift <= window
                            taps = phase[SUBLANES * q:SUBLANES * q + CONV_ROW_CHUNK, :]
                            acc = acc + wdw_ref[k, :, lanes][None] * taps.reshape(tiles, SUBLANES, CONV_LANE_CHUNK)
                y_scr[r0:r0 + CONV_ROW_CHUNK, lanes] = acc.reshape(CONV_ROW_CHUNK, CONV_LANE_CHUNK)
        o_ref[:, p * part_rows:(p + 1) * part_rows, :] = _conv_tail(
            x, y_scr[p * part_rows:(p + 1) * part_rows, :], mod_ref, cg_ref, cb_ref, wpw2_ref, bpw2_ref, g_ref, b_ref)

    @pl.when(t == pl.num_programs(1) - 1)
    def _():
        st_ref[...] = buf_scr[rows + off:rows + CONV_HALO, :].reshape(st_ref.shape)

    buf_scr[0:CONV_HALO, :] = buf_scr[rows:rows + CONV_HALO, :]


def _conv_p_call(x, mod, layer, j, w_pw1, b_pw1, w_dw, b_dw, cln_g, cln_b, w_pw2, b_pw2, mix_g, mix_b,
                 tile_rows):
    n_b, seq, _ = x.shape
    n_t = seq // tile_rows
    xspec = pl.BlockSpec((1, tile_rows, D_MODEL), lambda b, t: (b, t, 0))
    return pl.pallas_call(
        _conv_p_kernel,
        grid=(n_b, n_t),
        in_specs=[xspec,
                  pl.BlockSpec((None, 1, 1, 6 * D_MODEL), lambda b, t: (layer, b, 0, 0)),
                  _layer_spec(w_pw1, j), _layer_spec(b_pw1, j), _layer_spec(w_dw, j), _layer_spec(b_dw, j),
                  _layer_spec(cln_g, j), _layer_spec(cln_b, j), _layer_spec(w_pw2, j), _layer_spec(b_pw2, j),
                  _layer_spec(mix_g, layer), _layer_spec(mix_b, layer)],
        out_specs=[xspec, pl.BlockSpec((1, CONV_CTX, D_MODEL), lambda b, t: (b, 0, 0))],
        out_shape=[jax.ShapeDtypeStruct(x.shape, F32),
                   jax.ShapeDtypeStruct((n_b, CONV_CTX, D_MODEL), F32)],
        scratch_shapes=[pltpu.VMEM((tile_rows + CONV_HALO, D_MODEL), F32),
                        pltpu.VMEM((tile_rows, D_MODEL), F32)],
        compiler_params=_params("arbitrary", "arbitrary"),
        name=f"conv_p_l{layer}",
    )(x, mod, w_pw1, b_pw1, w_dw, b_dw, cln_g, cln_b, w_pw2, b_pw2, mix_g, mix_b)


def _conv_s_kernel(x_ref, mod_ref, st_ref, wpw1_ref, bpw1_ref, coef_ref, bdw_ref, cg_ref, cb_ref,
                   wpw2_ref, bpw2_ref, g_ref, b_ref, o_ref, a_ref):
    x = x_ref[...]
    g, r, _ = x.shape
    h = _modulated_rows(x, mod_ref, 0, 1)
    a = _glu_rows(h, wpw1_ref, bpw1_ref).reshape(g, r, D_MODEL)
    a_ref[...] = a
    y = jnp.broadcast_to(bdw_ref[...], (g, r, D_MODEL))
    for i in range(CONV_CTX):
        y = y + coef_ref[i] * st_ref[:, i:i + 1, :]
    for i in range(r):
        y = y + coef_ref[CONV_CTX + i] * a[:, i:i + 1, :]
    o_ref[...] = _conv_tail(x, y.reshape(g * r, D_MODEL), mod_ref, cg_ref, cb_ref, wpw2_ref, bpw2_ref,
                            g_ref, b_ref)


def _conv_s_call(x, mod, layer, j, state, w_pw1, b_pw1, coef, b_dw, cln_g, cln_b, w_pw2, b_pw2, mix_g, mix_b,
                 tile_rows):
    s = _Stream(x, tile_rows)
    st_spec = pl.BlockSpec((None, s.g, CONV_CTX, D_MODEL), lambda i: (j, i, 0, 0))
    return pl.pallas_call(
        _conv_s_kernel,
        grid=(s.n_tiles,),
        in_specs=[s.x_spec(), s.mod_spec(layer), st_spec, _layer_spec(w_pw1, j), _layer_spec(b_pw1, j),
                  _const_spec(coef.shape), _layer_spec(b_dw, j), _layer_spec(cln_g, j), _layer_spec(cln_b, j),
                  _layer_spec(w_pw2, j), _layer_spec(b_pw2, j), _layer_spec(mix_g, layer),
                  _layer_spec(mix_b, layer)],
        out_specs=[s.x_spec(), s.x_spec()],
        out_shape=[s.x_shape(), s.x_shape()],
        compiler_params=_params("arbitrary"),
        name=f"conv_s_l{layer}",
    )(x, mod, state, w_pw1, b_pw1, coef, b_dw, cln_g, cln_b, w_pw2, b_pw2, mix_g, mix_b)


def _rope_tile(x, cos, sin_signed):
    lane = lax.broadcasted_iota(jnp.int32, x.shape, 1)
    upper = pltpu.roll(x, LANES - MLA_ROPE // 2, 1)
    lower = pltpu.roll(x, MLA_ROPE // 2, 1)
    swapped = jnp.where((lane % MLA_ROPE) < MLA_ROPE // 2, upper, lower)
    return x * cos + swapped * sin_signed


MLA_QK = MLA_KV_RANK + MLA_ROPE


def _mla_proj_kernel(merged, x_ref, mod_ref, wdown_ref, gq_ref, gkv_ref, wuq_ref, wuk_ref, cos_ref, sin_ref,
                     *out_refs):
    if merged:
        q_ref, ckv_ref, kr_ref, kcat_ref = out_refs
        qa_dst = lambda hd: q_ref.at[hd, :, :MLA_KV_RANK]
        qr_dst = lambda hd: q_ref.at[hd, :, MLA_KV_RANK:]
        q_dtype = q_ref.dtype
    else:
        qa_ref, qr_ref, ckv_ref, kr_ref = out_refs
        qa_dst = lambda hd: qa_ref.at[hd]
        qr_dst = lambda hd: qr_ref.at[hd]
        q_dtype = qa_ref.dtype
    x = x_ref[...]
    h = _modulated_rows(x, mod_ref, 0, 1)
    d = _dot(h, wdown_ref[...])
    q_lat = _rms_norm(d[:, :MLA_Q_RANK], gq_ref[...])
    ckv = _rms_norm(d[:, MLA_Q_RANK:MLA_Q_RANK + MLA_KV_RANK], gkv_ref[...])
    cos = cos_ref[...]
    sin = sin_ref[...]
    kr = _rope_tile(d[:, MLA_Q_RANK + MLA_KV_RANK:], cos, sin)[:, :MLA_ROPE]
    ckv_ref[...] = ckv
    kr_ref[...] = kr
    if merged:
        kcat_ref[:, :MLA_KV_RANK] = ckv.astype(BF16)
        kcat_ref[:, MLA_KV_RANK:] = kr.astype(BF16)
    q = _dot(q_lat.astype(BF16), wuq_ref[...])
    n_nope = MLA_HEADS * MLA_NOPE
    for hd in range(MLA_HEADS):
        qn = q[:, hd * MLA_NOPE:(hd + 1) * MLA_NOPE].astype(BF16)
        qa_dst(hd)[...] = (_dot(qn, wuk_ref[hd]) * MLA_SCALE).astype(q_dtype)
    for t in range(MLA_HEADS * MLA_ROPE // LANES):
        rot = _rope_tile(q[:, n_nope + t * LANES:n_nope + (t + 1) * LANES], cos, sin) * MLA_SCALE
        qr_dst(2 * t)[...] = rot[:, :MLA_ROPE].astype(q_dtype)
        qr_dst(2 * t + 1)[...] = rot[:, MLA_ROPE:].astype(q_dtype)


def _mla_proj_call(x, mod, layer, j, w_down, g_q, g_kv, w_uq, w_uk, cos, sin, tile_rows, merged, pos_tiles):
    s = _Stream(x, tile_rows)
    n = s.n_tokens
    if pos_tiles == 1:
        pos_spec = pl.BlockSpec((s.rows, LANES), lambda i: (0, 0))
    else:
        pos_spec = pl.BlockSpec((s.rows, LANES), lambda i: (i % pos_tiles, 0))
    if merged:
        out_specs = [pl.BlockSpec((MLA_HEADS, s.rows, MLA_QK), lambda i: (0, i, 0))]
        out_shape = [jax.ShapeDtypeStruct((MLA_HEADS, n, MLA_QK), BF16)]
    else:
        out_specs = [pl.BlockSpec((MLA_HEADS, s.rows, MLA_KV_RANK), lambda i: (0, i, 0)),
                     pl.BlockSpec((MLA_HEADS, s.rows, MLA_ROPE), lambda i: (0, i, 0))]
        out_shape = [jax.ShapeDtypeStruct((MLA_HEADS, n, MLA_KV_RANK), F32),
                     jax.ShapeDtypeStruct((MLA_HEADS, n, MLA_ROPE), F32)]
    out_specs += [s.rows_spec(MLA_KV_RANK), s.rows_spec(MLA_ROPE)]
    out_shape += [jax.ShapeDtypeStruct((n, MLA_KV_RANK), F32), jax.ShapeDtypeStruct((n, MLA_ROPE), F32)]
    if merged:
        out_specs.append(s.rows_spec(MLA_QK))
        out_shape.append(jax.ShapeDtypeStruct((n, MLA_QK), BF16))
    return pl.pallas_call(
        functools.partial(_mla_proj_kernel, merged),
        grid=(s.n_tiles,),
        in_specs=[s.x_spec(), s.mod_spec(layer), _layer_spec(w_down, j), _layer_spec(g_q, j),
                  _layer_spec(g_kv, j), _layer_spec(w_uq, j), _layer_spec(w_uk, j), pos_spec, pos_spec],
        out_specs=out_specs,
        out_shape=out_shape,
        compiler_params=_params("arbitrary"),
        name=f"mla_proj_l{layer}_g{s.g}",
    )(x, mod, w_down, g_q, g_kv, w_uq, w_uk, cos, sin)


ATTN_Q_TILE = 512
ATTN_K_TILE = 512
ATTN_ROW_PARTS = 8


def _attn_p_kernel(q_ref, k_ref, o_ref, m_scr, l_scr, acc_scr, s_scr):
    i = pl.program_id(1)
    heads, tq, _ = q_ref.shape
    rows = heads * tq
    tk = ATTN_K_TILE
    m_scr[...] = jnp.full(m_scr.shape, NEG_BIG, F32)
    l_scr[...] = jnp.zeros(l_scr.shape, F32)
    acc_scr[...] = jnp.zeros(acc_scr.shape, F32)
    part_rows = rows // ATTN_ROW_PARTS
    parts = [slice(p * part_rows, (p + 1) * part_rows) for p in range(ATTN_ROW_PARTS)]

    def q_part(rs):
        return q_ref[rs.start // tq:rs.stop // tq].reshape(part_rows, MLA_QK)

    def keys(jb):
        return k_ref[pl.ds(pl.multiple_of(jb * tk, tk), tk), :]

    def absorb(rs, s, vb, masked):
        if masked:
            q_pos = lax.broadcasted_iota(jnp.int32, s.shape, 0) % tq
            k_pos = lax.broadcasted_iota(jnp.int32, s.shape, 1)
            s = jnp.where(k_pos <= q_pos, s, NEG_BIG)
        m_prev = m_scr[rs, :]
        m_next = jnp.maximum(m_prev, jnp.max(s, axis=1, keepdims=True))
        p = jnp.exp(s - jnp.concatenate([m_next] * (tk // LANES), axis=1))
        alpha = jnp.exp(m_prev - m_next)
        l_scr[rs, :] = alpha * l_scr[rs, :] + jnp.sum(p, axis=1, keepdims=True)
        m_scr[rs, :] = m_next
        acc_scr[rs, :] = (acc_scr[rs, :] * jnp.concatenate([alpha] * (MLA_KV_RANK // LANES), axis=1)
                          + _dot(p.astype(BF16), vb))

    kb0 = keys(0)
    for rs in parts:
        s_scr[rs, :] = _dot_nt(q_part(rs), kb0)

    def body(jb, carry):
        vb = keys(jb)[:, :MLA_KV_RANK]
        kb_next = keys(jb + 1)
        for rs in parts:
            s = s_scr[rs, :]
            s_scr[rs, :] = _dot_nt(q_part(rs), kb_next)
            absorb(rs, s, vb, False)
        return carry

    lax.fori_loop(0, i, body, 0)
    vb = keys(i)[:, :MLA_KV_RANK]
    for rs in parts:
        absorb(rs, s_scr[rs, :], vb, True)
    inv = 1.0 / l_scr[...]
    ctx = acc_scr[...] * jnp.concatenate([inv] * (MLA_KV_RANK // LANES), axis=1)
    for hd in range(heads):
        o_ref[:, hd * MLA_KV_RANK:(hd + 1) * MLA_KV_RANK] = ctx[hd * tq:(hd + 1) * tq].astype(o_ref.dtype)


def _attn_p_call(q, kcat, n_batch, seq):
    assert ATTN_Q_TILE == ATTN_K_TILE
    nq = seq // ATTN_Q_TILE
    rows = MLA_HEADS * ATTN_Q_TILE
    return pl.pallas_call(
        _attn_p_kernel,
        grid=(n_batch, nq),
        in_specs=[pl.BlockSpec((MLA_HEADS, ATTN_Q_TILE, MLA_QK), lambda b, i: (0, b * nq + i, 0)),
                  pl.BlockSpec((seq, MLA_QK), lambda b, i: (b, 0))],
        out_specs=pl.BlockSpec((ATTN_Q_TILE, MLA_HEADS * MLA_KV_RANK), lambda b, i: (b * nq + i, 0)),
        out_shape=jax.ShapeDtypeStruct((n_batch * seq, MLA_HEADS * MLA_KV_RANK), BF16),
        scratch_shapes=[pltpu.VMEM((rows, LANES), F32), pltpu.VMEM((rows, LANES), F32),
                        pltpu.VMEM((rows, MLA_KV_RANK), F32), pltpu.VMEM((rows, ATTN_K_TILE), F32)],
        compiler_params=_params("arbitrary", "arbitrary"),
        name="mla_attn_p",
    )(q, kcat)


ATTN_S_SPLITS = 4


def _attn_s_kernel(j, pt_ref, qa_ref, qr_ref, kn_ref, rn_ref, cache_k, cache_rt, o_ref, kbuf, rbuf, sems):
    b = pl.program_id(0)
    nb = pl.num_programs(0)
    n_pages = pt_ref.shape[1]
    page = cache_k.shape[2]
    slot = b % 2

    def copies(pg, sl, p):
        keys = pl.ds(p * page, page)
        return (pltpu.make_async_copy(cache_k.at[j, pg], kbuf.at[sl, keys, :], sems.at[0, sl]),
                pltpu.make_async_copy(cache_rt.at[j, pg], rbuf.at[sl, :, keys], sems.at[1, sl]))

    def start_all(bi, sl):
        for p in range(n_pages):
            for c in copies(pt_ref[bi, p], sl, p):
                c.start()

    def wait_all(sl):
        for p in range(n_pages):
            for c in copies(0, sl, p):
                c.wait()

    @pl.when(b == 0)
    def _():
        start_all(0, 0)

    @pl.when(b + 1 < nb)
    def _():
        start_all(b + 1, 1 - slot)

    wait_all(slot)

    heads, t, _ = qa_ref.shape
    rows = heads * t
    qa = qa_ref[...].reshape(rows, MLA_KV_RANK).astype(BF16)
    qr = qr_ref[...].reshape(rows, MLA_ROPE).astype(BF16)
    kn = kn_ref[...].astype(BF16)
    rn = rn_ref[...].astype(BF16)

    s_new = _dot_nt(qa, kn) + _dot_nt(qr, rn)
    q_pos = lax.broadcasted_iota(jnp.int32, s_new.shape, 0) % t
    k_pos = lax.broadcasted_iota(jnp.int32, s_new.shape, 1)
    s_new = jnp.where(k_pos <= q_pos, s_new, NEG_BIG)
    m_new = jnp.max(s_new, axis=1, keepdims=True)
    p_new = jnp.exp(s_new - m_new)
    parts = [(m_new, jnp.sum(p_new, axis=1, keepdims=True), _dot(p_new.astype(BF16), kn))]

    span = n_pages * page // ATTN_S_SPLITS
    chunks = [slice(c * span, (c + 1) * span) for c in range(ATTN_S_SPLITS)]
    kps = [kbuf[slot, keys, :].astype(BF16) for keys in chunks]
    ss = [_dot_nt(qa, kp) + _dot(qr, rbuf[slot, :, keys].astype(BF16)) for kp, keys in zip(kps, chunks)]
    ms = [jnp.max(s, axis=1, keepdims=True) for s in ss]
    ps = [jnp.exp(s - m_c) for s, m_c in zip(ss, ms)]
    for m_c, p, kp in zip(ms, ps, kps):
        parts.append((m_c, jnp.sum(p, axis=1, keepdims=True), _dot(p.astype(BF16), kp)))

    m = functools.reduce(jnp.maximum, [pt[0] for pt in parts])
    scales = [jnp.exp(pt[0] - m) for pt in parts]
    denom = sum(sc * pt[1] for sc, pt in zip(scales, parts))
    ctx = sum(sc * pt[2] for sc, pt in zip(scales, parts)) / denom
    for hd in range(heads):
        o_ref[:, hd * MLA_KV_RANK:(hd + 1) * MLA_KV_RANK] = ctx[hd * t:(hd + 1) * t]


def _attn_s_call(page_table, qa, qr, ckv_new, kr_new, cache_k, cache_r, j, n_batch, t):
    past = page_table.shape[1] * cache_k.shape[2]
    grid_spec = pltpu.PrefetchScalarGridSpec(
        num_scalar_prefetch=1,
        grid=(n_batch,),
        in_specs=[pl.BlockSpec((MLA_HEADS, t, MLA_KV_RANK), lambda b, pt: (0, b, 0)),
                  pl.BlockSpec((MLA_HEADS, t, MLA_ROPE), lambda b, pt: (0, b, 0)),
                  pl.BlockSpec((t, MLA_KV_RANK), lambda b, pt: (b, 0)),
                  pl.BlockSpec((t, MLA_ROPE), lambda b, pt: (b, 0)),
                  pl.BlockSpec(memory_space=pl.ANY),
                  pl.BlockSpec(memory_space=pl.ANY)],
        out_specs=pl.BlockSpec((t, MLA_HEADS * MLA_KV_RANK), lambda b, pt: (b, 0)),
        scratch_shapes=[pltpu.VMEM((2, past, MLA_KV_RANK), F32),
                        pltpu.VMEM((2, MLA_ROPE, past), F32),
                        pltpu.SemaphoreType.DMA((2, 2))],
    )
    return pl.pallas_call(
        functools.partial(_attn_s_kernel, j),
        grid_spec=grid_spec,
        out_shape=jax.ShapeDtypeStruct((n_batch * t, MLA_HEADS * MLA_KV_RANK), F32),
        compiler_params=_params("arbitrary"),
        name="mla_attn_s",
    )(page_table, qa, qr, ckv_new, kr_new, cache_k, cache_r)


def _mla_out_kernel(x_ref, mod_ref, ctx_ref, wuv_ref, wo_ref, g_ref, b_ref, o_ref):
    x = x_ref[...]
    parts = []
    for hd in range(MLA_HEADS):
        c = ctx_ref[:, hd * MLA_KV_RANK:(hd + 1) * MLA_KV_RANK].astype(BF16)
        parts.append(_dot(c, wuv_ref[hd]).astype(BF16))
    o = jnp.concatenate(parts, axis=1)
    y = _dot(o, wo_ref[...])
    o_ref[...] = _residual_norm(x, y, _mod(mod_ref, 2), g_ref, b_ref)


def _mla_out_call(x, mod, layer, j, ctx, w_uv, w_o, mix_g, mix_b, tile_rows):
    s = _Stream(x, tile_rows)
    return pl.pallas_call(
        _mla_out_kernel,
        grid=(s.n_tiles,),
        in_specs=[s.x_spec(), s.mod_spec(layer), s.rows_spec(MLA_HEADS * MLA_KV_RANK),
                  _layer_spec(w_uv, j), _layer_spec(w_o, j), _layer_spec(mix_g, layer),
                  _layer_spec(mix_b, layer)],
        out_specs=s.x_spec(),
        out_shape=s.x_shape(),
        compiler_params=_params("arbitrary"),
        name=f"mla_out_l{layer}_g{s.g}",
    )(x, mod, ctx, w_uv, w_o, mix_g, mix_b)


def _row3(a):
    return a.reshape(a.shape[0], 1, a.shape[1])


def _rope_tables(pos):
    half = MLA_ROPE // 2
    inv = ROPE_THETA ** (-jnp.arange(half, dtype=F32) / half)
    ang = pos.astype(F32)[:, None] * inv[None, :]
    cos, sin = jnp.cos(ang), jnp.sin(ang)
    cos64 = jnp.concatenate([cos, cos], -1)
    sin64 = jnp.concatenate([-sin, sin], -1)
    return jnp.tile(cos64, (1, LANES // MLA_ROPE)), jnp.tile(sin64, (1, LANES // MLA_ROPE))


def kernel(x_prompt, x_sample, cache_ckv, cache_krope, state_conv, page_table, c_prompt, c_sample,
           ada_w, ada_b, ln_mix_g, ln_mix_b, ln_ffn_g, ln_ffn_b, ffn_w_gate, ffn_w_up, ffn_w_down,
           gmlp_w_in, gmlp_ln_g, gmlp_ln_b, gmlp_w_s, gmlp_b_s, gmlp_w_out,
           conv_w_pw1, conv_b_pw1, conv_w_dw, conv_b_dw, conv_ln_g, conv_ln_b, conv_w_pw2, conv_b_pw2,
           mla_w_down, mla_g_q, mla_g_kv, mla_w_uq, mla_w_uk, mla_w_uv, mla_w_o):
    n_b, seq, _ = x_prompt.shape
    n_db, dec_seq, _ = x_sample.shape
    past_len = page_table.shape[1] * cache_ckv.shape[2]

    c_p = jnp.pad(c_prompt, ((0, SUBLANES - n_b), (0, 0)))
    mod_p, mod_s = _ada_call(c_p, c_sample, ada_w, ada_b)
    mod_p = mod_p.reshape(DEPTH, SUBLANES, 1, 6 * D_MODEL)
    mod_s = mod_s.reshape(DEPTH, n_db, 1, 6 * D_MODEL)

    ln_mix_g, ln_mix_b, ln_ffn_g, ln_ffn_b = map(_row3, (ln_mix_g, ln_mix_b, ln_ffn_g, ln_ffn_b))
    wg, wu, wd = (w.astype(BF16) for w in (ffn_w_gate, ffn_w_up, ffn_w_down))

    g_w_in, g_w_out = gmlp_w_in.astype(BF16), gmlp_w_out.astype(BF16)
    g_ln_g, g_ln_b = _row3(gmlp_ln_g), _row3(gmlp_ln_b)
    reps = GMLP_CHUNK // dec_seq
    eye = jnp.eye(reps, dtype=F32)

    c_w_pw1, c_w_pw2 = conv_w_pw1.astype(BF16), conv_w_pw2.astype(BF16)
    c_b_pw1, c_b_dw, c_ln_g, c_ln_b, c_b_pw2 = map(_row3, (conv_b_pw1, conv_b_dw, conv_ln_g, conv_ln_b,
                                                            conv_b_pw2))
    c_w_dw8 = jnp.broadcast_to(conv_w_dw[:, :, None, :], conv_w_dw.shape[:2] + (SUBLANES, D_MODEL))
    tap = jnp.arange(CONV_CTX + dec_seq)[:, None] - jnp.arange(dec_seq)[None, :]
    tap_ok = (tap >= 0) & (tap < CONV_WIDTH)
    conv_coef = jnp.where(tap_ok[None, :, :, None], conv_w_dw[:, jnp.clip(tap, 0, CONV_WIDTH - 1)], 0.0)

    n_mla = mla_w_down.shape[0]
    m_w_down = jnp.pad(mla_w_down, ((0, 0), (0, 0), (0, MLA_DOWN_PAD - mla_w_down.shape[-1]))).astype(BF16)
    w_uq = mla_w_uq.reshape(n_mla, MLA_Q_RANK, MLA_HEADS, MLA_NOPE + MLA_ROPE)
    m_w_uq = jnp.concatenate([w_uq[..., :MLA_NOPE].reshape(n_mla, MLA_Q_RANK, -1),
                              w_uq[..., MLA_NOPE:].reshape(n_mla, MLA_Q_RANK, -1)], -1).astype(BF16)
    m_w_uk = mla_w_uk.transpose(0, 2, 3, 1).astype(BF16)
    m_w_uv = mla_w_uv.transpose(0, 2, 1, 3).astype(BF16)
    m_w_o = mla_w_o.astype(BF16)
    m_g_q, m_g_kv = _row3(mla_g_q), _row3(mla_g_kv)
    cos_p, sin_p = _rope_tables(jnp.arange(seq))
    cos_s, sin_s = _rope_tables(past_len + jnp.arange(dec_seq))

    tile_p, tile_s = 512, 512
    gm_tile = 512
    conv_tile = 512
    mla_tile = 512
    s_reps = mla_tile // dec_seq
    cos_s, sin_s = jnp.tile(cos_s, (s_reps, 1)), jnp.tile(sin_s, (s_reps, 1))

    xp, xs = x_prompt, x_sample
    ckv_p_rows, kr_p_rows, ckv_s_rows, kr_s_rows = [], [], [], []
    conv_p_states, conv_s_states, gmlp_v_rows = [], [], []
    for i in range(DEPTH):
        kind, j = i % N_MIXERS, i // N_MIXERS
        if kind == 0:
            wmix_p = gmlp_w_s[j].astype(BF16)
            bmix_p = gmlp_b_s[j].T
            corner = gmlp_w_s[j, :, :dec_seq, :dec_seq]
            wmix_s = jnp.einsum('ab,gts->gatbs', eye, corner).reshape(
                GMLP_GROUPS, GMLP_CHUNK, GMLP_CHUNK).astype(BF16)
            bmix_s = jnp.tile(gmlp_b_s[j, :, :dec_seq].T, (reps, 1))
            xp, _ = _gmlp_call(xp, mod_p, i, j, g_w_in, g_ln_g, g_ln_b, wmix_p, bmix_p, g_w_out,
                               ln_mix_g, ln_mix_b, gm_tile, False)
            xs, v_s = _gmlp_call(xs, mod_s, i, j, g_w_in, g_ln_g, g_ln_b, wmix_s, bmix_s, g_w_out,
                                 ln_mix_g, ln_mix_b, gm_tile, True)
            gmlp_v_rows.append(v_s)
        elif kind == 1:
            xp, st_p = _conv_p_call(xp, mod_p, i, j, c_w_pw1, c_b_pw1, c_w_dw8, c_b_dw, c_ln_g, c_ln_b,
                                    c_w_pw2, c_b_pw2, ln_mix_g, ln_mix_b, conv_tile)
            xs_new, a_s = _conv_s_call(xs, mod_s, i, j, state_conv, c_w_pw1, c_b_pw1, conv_coef[j], c_b_dw,
                                       c_ln_g, c_ln_b, c_w_pw2, c_b_pw2, ln_mix_g, ln_mix_b, conv_tile)
            xs = xs_new
            conv_p_states.append(st_p)
            conv_s_states.append(jnp.concatenate([state_conv[j][:, dec_seq:], a_s], axis=1))
        else:
            q_p, ckv, kr, kcat = _mla_proj_call(
                xp, mod_p, i, j, m_w_down, m_g_q, m_g_kv, m_w_uq, m_w_uk, cos_p, sin_p, mla_tile, True,
                seq // mla_tile)
            ctx_p = _attn_p_call(q_p, kcat, n_b, seq)
            xp = _mla_out_call(xp, mod_p, i, j, ctx_p, m_w_uv, m_w_o, ln_mix_g, ln_mix_b, mla_tile)
            ckv_p_rows.append(ckv.reshape(n_b, seq, MLA_KV_RANK))
            kr_p_rows.append(kr.reshape(n_b, seq, MLA_ROPE))
            qa, qr, ckv, kr = _mla_proj_call(
                xs, mod_s, i, j, m_w_down, m_g_q, m_g_kv, m_w_uq, m_w_uk, cos_s, sin_s, mla_tile, False, 1)
            ctx_s = _attn_s_call(page_table, qa, qr, ckv, kr, cache_ckv, jnp.swapaxes(cache_krope, 2, 3), j,
                                 n_db, dec_seq)
            xs = _mla_out_call(xs, mod_s, i, j, ctx_s, m_w_uv, m_w_o, ln_mix_g, ln_mix_b, mla_tile)
            ckv_s_rows.append(ckv.reshape(n_db, dec_seq, MLA_KV_RANK))
            kr_s_rows.append(kr.reshape(n_db, dec_seq, MLA_ROPE))
        xp = _ffn_call(xp, mod_p, i, wg, wu, wd, ln_ffn_g, ln_ffn_b, tile_p)
        xs = _ffn_call(xs, mod_s, i, wg, wu, wd, ln_ffn_g, ln_ffn_b, tile_s)
    return (xp, xs, jnp.stack(ckv_p_rows), jnp.stack(kr_p_rows), jnp.stack(ckv_s_rows), jnp.stack(kr_s_rows),
            jnp.stack(conv_p_states), jnp.stack(conv_s_states), jnp.stack(gmlp_v_rows))
```

```python
import functools
import math

import jax
import jax.numpy as jnp
from jax import lax
from jax.experimental import pallas as pl
from jax.experimental.pallas import tpu as pltpu

F32 = jnp.float32
BF16 = jnp.bfloat16

D_MODEL = 1024
DEPTH = 4
N_MIXERS = 3
ALPHA = (2.0 * DEPTH) ** 0.25
LN_EPS = 1e-5
RMS_EPS = 1e-6

GMLP_DIM = 2 * D_MODEL
GMLP_GROUPS = 8
GMLP_CHUNK = 128
GMLP_GROUP_WIDTH = GMLP_DIM // GMLP_GROUPS

CONV_WIDTH = 31
CONV_CTX = CONV_WIDTH - 1

MLA_HEADS = 8
MLA_Q_RANK = 384
MLA_KV_RANK = 256
MLA_NOPE = 128
MLA_ROPE = 64
MLA_V = 128
ROPE_THETA = 10000.0
MLA_SCALE = (MLA_NOPE + MLA_ROPE) ** -0.5
MLA_DOWN_PAD = MLA_Q_RANK + MLA_KV_RANK + 128

FFN_DIM = ((8 * D_MODEL + 3 * 256 - 1) // (3 * 256)) * 256

VMEM_LIMIT_BYTES = 56 * 1024 * 1024
LANES = 128
SUBLANES = 8

NEG_BIG = -1e30


def _params(*semantics):
    return pltpu.CompilerParams(dimension_semantics=semantics, vmem_limit_bytes=VMEM_LIMIT_BYTES)


def _dot(a, b):
    return jnp.dot(a, b, preferred_element_type=F32)


def _dot_nt(a, b):
    return lax.dot_general(a, b, (((1,), (1,)), ((), ())), preferred_element_type=F32)


def _layer_norm(x, g, b):
    mu = jnp.mean(x, -1, keepdims=True)
    xc = x - mu
    var = jnp.mean(xc * xc, -1, keepdims=True)
    return xc * lax.rsqrt(var + LN_EPS) * g + b


def _rms_norm(x, g):
    return x * lax.rsqrt(jnp.mean(x * x, -1, keepdims=True) + RMS_EPS) * g


def _mod(mod_ref, k, groups=slice(None)):
    return mod_ref[groups, :, k * D_MODEL:(k + 1) * D_MODEL]


def _modulated_rows(x, mod_ref, k_shift, k_scale, groups=slice(None)):
    g, r, d = x.shape
    h = x * (1.0 + _mod(mod_ref, k_scale, groups)) + _mod(mod_ref, k_shift, groups)
    return h.reshape(g * r, d).astype(BF16)


def _row_parts(x_ref, n):
    g, r, _ = x_ref.shape
    if g == 1:
        return [(slice(0, 1), slice(p * (r // n), (p + 1) * (r // n))) for p in range(n)]
    return [(slice(p * (g // n), (p + 1) * (g // n)), slice(0, r)) for p in range(n)]


def _residual_norm(x, y_rows, gate, g_ref, b_ref):
    return _layer_norm(ALPHA * x + gate * y_rows.reshape(x.shape), g_ref[...], b_ref[...])


def _ada_kernel(cp_ref, cs_ref, w_ref, b_ref, mp_ref, ms_ref):
    w = w_ref[...].astype(BF16)
    bias = b_ref[...]
    for c_ref, o_ref in ((cp_ref, mp_ref), (cs_ref, ms_ref)):
        c = c_ref[...]
        a = (c * jax.nn.sigmoid(c)).astype(BF16)
        o_ref[...] = _dot(a, w) + bias


def _ada_call(c_p, c_s, ada_w, ada_b):
    tn = 1536
    n_p, n_s = c_p.shape[0], c_s.shape[0]
    d6 = ada_w.shape[-1]
    return pl.pallas_call(
        _ada_kernel,
        grid=(DEPTH, d6 // tn),
        in_specs=[
            pl.BlockSpec((n_p, D_MODEL), lambda l, n: (0, 0)),
            pl.BlockSpec((n_s, D_MODEL), lambda l, n: (0, 0)),
            pl.BlockSpec((None, D_MODEL, tn), lambda l, n: (l, 0, n)),
            pl.BlockSpec((None, 1, tn), lambda l, n: (l, 0, n)),
        ],
        out_specs=[
            pl.BlockSpec((None, n_p, tn), lambda l, n: (l, 0, n)),
            pl.BlockSpec((None, n_s, tn), lambda l, n: (l, 0, n)),
        ],
        out_shape=[
            jax.ShapeDtypeStruct((DEPTH, n_p, d6), F32),
            jax.ShapeDtypeStruct((DEPTH, n_s, d6), F32),
        ],
        compiler_params=_params("arbitrary", "arbitrary"),
        name="ada_table",
    )(c_p, c_s, ada_w, ada_b.reshape(DEPTH, 1, d6))


class _Stream:
    def __init__(self, x, tile_rows, first=None):
        self.first = first
        self.n_groups, self.group_rows, _ = x.shape
        if self.group_rows >= tile_rows:
            assert self.group_rows % tile_rows == 0
            self.g, self.r = 1, tile_rows
            self.tiles_per_group = self.group_rows // tile_rows
        else:
            assert tile_rows % self.group_rows == 0
            self.g, self.r = tile_rows // self.group_rows, self.group_rows
            assert self.n_groups % self.g == 0
            self.tiles_per_group = 1
        self.rows = self.g * self.r
        self.n_tiles = (self.n_groups // self.g) * self.tiles_per_group
        self.n_tokens = self.n_groups * self.group_rows

    def _tile(self, i):
        return i if self.first is None else jnp.clip(i - self.first, 0, self.n_tiles - 1)

    def _gi(self, i):
        t = self._tile(i)
        if self.tiles_per_group == 1:
            return t, 0
        return t // self.tiles_per_group, t % self.tiles_per_group

    def x_spec(self, width=D_MODEL):
        return pl.BlockSpec((self.g, self.r, width), lambda i: (*self._gi(i), 0))

    def mod_spec(self, layer):
        return pl.BlockSpec((None, self.g, 1, 6 * D_MODEL), lambda i: (layer, self._gi(i)[0], 0, 0))

    def rows_spec(self, width):
        return pl.BlockSpec((self.rows, width), lambda i: (self._tile(i), 0))

    def x_shape(self, width=D_MODEL, dtype=F32):
        return jax.ShapeDtypeStruct((self.n_groups, self.group_rows, width), dtype)


def _const_spec(shape):
    zeros = (0,) * len(shape)
    return pl.BlockSpec(shape, lambda *_: zeros, pipeline_mode=pl.Buffered(1))


def _layer_spec(arr, layer):
    zeros = (0,) * (arr.ndim - 1)
    return pl.BlockSpec((None,) + arr.shape[1:], lambda *_: (layer,) + zeros,
                        pipeline_mode=pl.Buffered(1))


MXU_WIDTH = 256
FFN_SPLIT = (FFN_DIM // MXU_WIDTH + 1) // 2 * MXU_WIDTH


FFN_ROW_PARTS = 2
GMLP_ROW_PARTS = 2
CONV_ROW_PARTS = 2


def _ffn_tile(x_ref, mod_ref, wg_ref, wu_ref, wd_ref, g_ref, b_ref, o_ref, a_scr):
    parts = _row_parts(x_ref, FFN_ROW_PARTS)
    xs = [x_ref[gs, rs, :] for gs, rs in parts]
    hs = [_modulated_rows(x, mod_ref, 3, 4, gs) for x, (gs, _) in zip(xs, parts)]
    fs = []
    r0 = 0
    for h in hs:
        rows = h.shape[0]
        for sl in (slice(0, FFN_SPLIT), slice(FFN_SPLIT, FFN_DIM)):
            gate = _dot(h, wg_ref[:, sl])
            up = _dot(h, wu_ref[:, sl])
            a_scr[r0:r0 + rows, sl] = (gate * jax.nn.sigmoid(gate) * up).astype(BF16)
        fs.append(_dot(a_scr[r0:r0 + rows, :], wd_ref[...]))
        r0 += rows
    for x, f, (gs, rs) in zip(xs, fs, parts):
        o_ref[gs, rs, :] = _residual_norm(x, f, _mod(mod_ref, 5, gs), g_ref, b_ref)


def _ffn_kernel(n_p, xp_ref, modp_ref, xs_ref, mods_ref, wg_ref, wu_ref, wd_ref, g_ref, b_ref,
                op_ref, os_ref, a_scr):
    i = pl.program_id(0)

    @pl.when(i < n_p)
    def _():
        _ffn_tile(xp_ref, modp_ref, wg_ref, wu_ref, wd_ref, g_ref, b_ref, op_ref, a_scr)

    @pl.when(i >= n_p)
    def _():
        _ffn_tile(xs_ref, mods_ref, wg_ref, wu_ref, wd_ref, g_ref, b_ref, os_ref, a_scr)


def _ffn_call(xp, mod_p, xs, mod_s, layer, wg, wu, wd, ln_g, ln_b, tile_rows):
    sp = _Stream(xp, tile_rows, first=0)
    ss = _Stream(xs, tile_rows, first=sp.n_tiles)
    assert sp.rows == ss.rows
    return pl.pallas_call(
        functools.partial(_ffn_kernel, sp.n_tiles),
        grid=(sp.n_tiles + ss.n_tiles,),
        in_specs=[sp.x_spec(), sp.mod_spec(layer), ss.x_spec(), ss.mod_spec(layer),
                  _layer_spec(wg, layer), _layer_spec(wu, layer), _layer_spec(wd, layer),
                  _layer_spec(ln_g, layer), _layer_spec(ln_b, layer)],
        out_specs=[sp.x_spec(), ss.x_spec()],
        out_shape=[sp.x_shape(), ss.x_shape()],
        scratch_shapes=[pltpu.VMEM((sp.rows, FFN_DIM), BF16)],
        compiler_params=_params("arbitrary"),
        name=f"ffn_l{layer}",
    )(xp, mod_p, xs, mod_s, wg, wu, wd, ln_g, ln_b)


def _gelu(z):
    return 0.5 * z * (1.0 + lax.erf(z * (1.0 / math.sqrt(2.0))))


def _gmlp_tile(x_ref, mod_ref, win_ref, lng_ref, lnb_ref, wmix_ref, bmix_ref, wout_ref, g_ref, b_ref,
               o_ref, v_ref, um_scr):
    row_id = lax.broadcasted_iota(jnp.int32, (GMLP_CHUNK, GMLP_CHUNK), 0)
    col_id = lax.broadcasted_iota(jnp.int32, (GMLP_CHUNK, GMLP_CHUNK), 1)
    causal = row_id >= col_id
    parts = _row_parts(x_ref, GMLP_ROW_PARTS)
    xs = [x_ref[gs, rs, :] for gs, rs in parts]
    zs = [_dot(_modulated_rows(x, mod_ref, 0, 1, gs), win_ref[...]) for x, (gs, _) in zip(xs, parts)]
    ys = []
    r0 = 0
    for x, z, (gs, rs) in zip(xs, zs, parts):
        rows = z.shape[0]
        z = _gelu(z)
        u = z[:, :GMLP_DIM]
        v = _layer_norm(z[:, GMLP_DIM:], lng_ref[...], lnb_ref[...])
        if v_ref is not None:
            v_ref[gs, rs, :] = v.reshape(x.shape[:2] + (GMLP_DIM,))
        vb = v.astype(BF16)
        for gi in range(GMLP_GROUPS):
            w = jnp.where(causal, wmix_ref[gi], jnp.zeros((), BF16))
            bias = bmix_ref[:, gi:gi + 1]
            cols = slice(gi * GMLP_GROUP_WIDTH, (gi + 1) * GMLP_GROUP_WIDTH)
            for c in range(rows // GMLP_CHUNK):
                rws = slice(c * GMLP_CHUNK, (c + 1) * GMLP_CHUNK)
                mixed = _dot(w, vb[rws, cols]) + bias
                um_scr[r0 + c * GMLP_CHUNK:r0 + (c + 1) * GMLP_CHUNK, cols] = (u[rws, cols] * mixed).astype(BF16)
        ys.append(_dot(um_scr[r0:r0 + rows, :], wout_ref[...]))
        r0 += rows
    for x, y, (gs, rs) in zip(xs, ys, parts):
        o_ref[gs, rs, :] = _residual_norm(x, y, _mod(mod_ref, 2, gs), g_ref, b_ref)


def _gmlp_kernel(n_p, xp_ref, modp_ref, xs_ref, mods_ref, win_ref, lng_ref, lnb_ref, wmixp_ref, bmixp_ref,
                 wmixs_ref, bmixs_ref, wout_ref, g_ref, b_ref, op_ref, os_ref, v_ref, um_scr):
    i = pl.program_id(0)

    @pl.when(i < n_p)
    def _():
        _gmlp_tile(xp_ref, modp_ref, win_ref, lng_ref, lnb_ref, wmixp_ref, bmixp_ref, wout_ref, g_ref, b_ref,
                   op_ref, None, um_scr)

    @pl.when(i >= n_p)
    def _():
        _gmlp_tile(xs_ref, mods_ref, win_ref, lng_ref, lnb_ref, wmixs_ref, bmixs_ref, wout_ref, g_ref, b_ref,
                   os_ref, v_ref, um_scr)


def _gmlp_call(xp, mod_p, xs, mod_s, layer, j, w_in, ln_g, ln_b, wmix_p, bmix_p, wmix_s, bmix_s, w_out,
               mix_g, mix_b, tile_rows):
    sp = _Stream(xp, tile_rows, first=0)
    ss = _Stream(xs, tile_rows, first=sp.n_tiles)
    assert sp.rows == ss.rows
    return pl.pallas_call(
        functools.partial(_gmlp_kernel, sp.n_tiles),
        grid=(sp.n_tiles + ss.n_tiles,),
        in_specs=[sp.x_spec(), sp.mod_spec(layer), ss.x_spec(), ss.mod_spec(layer),
                  _layer_spec(w_in, j), _layer_spec(ln_g, j), _layer_spec(ln_b, j),
                  _const_spec(wmix_p.shape), _const_spec(bmix_p.shape),
                  _const_spec(wmix_s.shape), _const_spec(bmix_s.shape),
                  _layer_spec(w_out, j), _layer_spec(mix_g, layer), _layer_spec(mix_b, layer)],
        out_specs=[sp.x_spec(), ss.x_spec(), ss.x_spec(GMLP_DIM)],
        out_shape=[sp.x_shape(), ss.x_shape(), ss.x_shape(GMLP_DIM)],
        scratch_shapes=[pltpu.VMEM((sp.rows, GMLP_DIM), BF16)],
        compiler_params=_params("arbitrary"),
        name=f"gmlp_l{layer}",
    )(xp, mod_p, xs, mod_s, w_in, ln_g, ln_b, wmix_p, bmix_p, wmix_s, bmix_s, w_out, mix_g, mix_b)


CONV_HALO = 32
CONV_ROW_CHUNK = 128
CONV_LANE_CHUNK = 256


def _glu_rows(h, wpw1_ref, bpw1_ref):
    a = _dot(h, wpw1_ref[...]) + bpw1_ref[...]
    return a[:, :D_MODEL] * jax.nn.sigmoid(a[:, D_MODEL:])


def _conv_tail(x, y, mod_ref, cg_ref, cb_ref, wpw2_ref, bpw2_ref, g_ref, b_ref):
    y = _layer_norm(y, cg_ref[...], cb_ref[...])
    y = (y * jax.nn.sigmoid(y)).astype(BF16)
    out = _dot(y, wpw2_ref[...]) + bpw2_ref[...]
    return _residual_norm(x, out, _mod(mod_ref, 2), g_ref, b_ref)


def _conv_p_kernel(x_ref, mod_ref, wpw1_ref, bpw1_ref, wdw_ref, bdw_ref, cg_ref, cb_ref, wpw2_ref, bpw2_ref,
                   g_ref, b_ref, o_ref, st_ref, buf_scr, y_scr):
    t = pl.program_id(1)
    rows = x_ref.shape[1]
    part_rows = rows // CONV_ROW_PARTS
    off = CONV_HALO - CONV_CTX

    @pl.when(t == 0)
    def _():
        buf_scr[0:CONV_HALO, :] = jnp.zeros((CONV_HALO, D_MODEL), F32)

    xs = [x_ref[:, p * part_rows:(p + 1) * part_rows, :] for p in range(CONV_ROW_PARTS)]
    for p, x in enumerate(xs):
        h = _modulated_rows(x, mod_ref, 0, 1)
        buf_scr[CONV_HALO + p * part_rows:CONV_HALO + (p + 1) * part_rows, :] = _glu_rows(h, wpw1_ref, bpw1_ref)

    n_q = (CONV_WIDTH + SUBLANES - 1) // SUBLANES
    window = CONV_ROW_CHUNK + CONV_HALO
    tiles = CONV_ROW_CHUNK // SUBLANES
    for p, x in enumerate(xs):
        for lc in range(D_MODEL // CONV_LANE_CHUNK):
            lanes = slice(lc * CONV_LANE_CHUNK, (lc + 1) * CONV_LANE_CHUNK)
            for rc in range(part_rows // CONV_ROW_CHUNK):
                r0 = p * part_rows + rc * CONV_ROW_CHUNK
                win = buf_scr[r0:r0 + window, lanes]
                acc = jnp.broadcast_to(bdw_ref[:, lanes][None], (tiles, SUBLANES, CONV_LANE_CHUNK))
                for r in range(SUBLANES):
                    shift = off + r
                    phase = win if shift % window == 0 else pltpu.roll(win, window - shift, 0)
                    for q in range(n_q):
                        k = SUBLANES * q + r
                        if k < CONV_WIDTH:
                            assert SUBLANES * q + CONV_ROW_CHUNK + shift <= window
                            taps = phase[SUBLANES * q:SUBLANES * q + CONV_ROW_CHUNK, :]
                            acc = acc + wdw_ref[k, :, lanes][None] * taps.reshape(tiles, SUBLANES, CONV_LANE_CHUNK)
                y_scr[r0:r0 + CONV_ROW_CHUNK, lanes] = acc.reshape(CONV_ROW_CHUNK, CONV_LANE_CHUNK)
        o_ref[:, p * part_rows:(p + 1) * part_rows, :] = _conv_tail(
            x, y_scr[p * part_rows:(p + 1) * part_rows, :], mod_ref, cg_ref, cb_ref, wpw2_ref, bpw2_ref, g_ref, b_ref)

    @pl.when(t == pl.num_programs(1) - 1)
    def _():
        st_ref[...] = buf_scr[rows + off:rows + CONV_HALO, :].reshape(st_ref.shape)

    buf_scr[0:CONV_HALO, :] = buf_scr[rows:rows + CONV_HALO, :]


def _conv_p_call(x, mod, layer, j, w_pw1, b_pw1, w_dw, b_dw, cln_g, cln_b, w_pw2, b_pw2, mix_g, mix_b,
                 tile_rows):
    n_b, seq, _ = x.shape
    n_t = seq // tile_rows
    xspec = pl.BlockSpec((1, tile_rows, D_MODEL), lambda b, t: (b, t, 0))
    return pl.pallas_call(
        _conv_p_kernel,
        grid=(n_b, n_t),
        in_specs=[xspec,
                  pl.BlockSpec((None, 1, 1, 6 * D_MODEL), lambda b, t: (layer, b, 0, 0)),
                  _layer_spec(w_pw1, j), _layer_spec(b_pw1, j), _layer_spec(w_dw, j), _layer_spec(b_dw, j),
                  _layer_spec(cln_g, j), _layer_spec(cln_b, j), _layer_spec(w_pw2, j), _layer_spec(b_pw2, j),
                  _layer_spec(mix_g, layer), _layer_spec(mix_b, layer)],
        out_specs=[xspec, pl.BlockSpec((1, CONV_CTX, D_MODEL), lambda b, t: (b, 0, 0))],
        out_shape=[jax.ShapeDtypeStruct(x.shape, F32),
                   jax.ShapeDtypeStruct((n_b, CONV_CTX, D_MODEL), F32)],
        scratch_shapes=[pltpu.VMEM((tile_rows + CONV_HALO, D_MODEL), F32),
                        pltpu.VMEM((tile_rows, D_MODEL), F32)],
        compiler_params=_params("arbitrary", "arbitrary"),
        name=f"conv_p_l{layer}",
    )(x, mod, w_pw1, b_pw1, w_dw, b_dw, cln_g, cln_b, w_pw2, b_pw2, mix_g, mix_b)


def _conv_s_kernel(x_ref, mod_ref, st_ref, wpw1_ref, bpw1_ref, coef_ref, bdw_ref, cg_ref, cb_ref,
                   wpw2_ref, bpw2_ref, g_ref, b_ref, o_ref, a_ref):
    x = x_ref[...]
    g, r, _ = x.shape
    h = _modulated_rows(x, mod_ref, 0, 1)
    a = _glu_rows(h, wpw1_ref, bpw1_ref).reshape(g, r, D_MODEL)
    a_ref[...] = a
    y = jnp.broadcast_to(bdw_ref[...], (g, r, D_MODEL))
    for i in range(CONV_CTX):
        y = y + coef_ref[i] * st_ref[:, i:i + 1, :]
    for i in range(r):
        y = y + coef_ref[CONV_CTX + i] * a[:, i:i + 1, :]
    o_ref[...] = _conv_tail(x, y.reshape(g * r, D_MODEL), mod_ref, cg_ref, cb_ref, wpw2_ref, bpw2_ref,
                            g_ref, b_ref)


def _conv_s_call(x, mod, layer, j, state, w_pw1, b_pw1, coef, b_dw, cln_g, cln_b, w_pw2, b_pw2, mix_g, mix_b,
                 tile_rows):
    s = _Stream(x, tile_rows)
    st_spec = pl.BlockSpec((None, s.g, CONV_CTX, D_MODEL), lambda i: (j, i, 0, 0))
    return pl.pallas_call(
        _conv_s_kernel,
        grid=(s.n_tiles,),
        in_specs=[s.x_spec(), s.mod_spec(layer), st_spec, _layer_spec(w_pw1, j), _layer_spec(b_pw1, j),
                  _const_spec(coef.shape), _layer_spec(b_dw, j), _layer_spec(cln_g, j), _layer_spec(cln_b, j),
                  _layer_spec(w_pw2, j), _layer_spec(b_pw2, j), _layer_spec(mix_g, layer),
                  _layer_spec(mix_b, layer)],
        out_specs=[s.x_spec(), s.x_spec()],
        out_shape=[s.x_shape(), s.x_shape()],
        compiler_params=_params("arbitrary"),
        name=f"conv_s_l{layer}",
    )(x, mod, state, w_pw1, b_pw1, coef, b_dw, cln_g, cln_b, w_pw2, b_pw2, mix_g, mix_b)


def _rope_tile(x, cos, sin_signed):
    lane = lax.broadcasted_iota(jnp.int32, x.shape, 1)
    upper = pltpu.roll(x, LANES - MLA_ROPE // 2, 1)
    lower = pltpu.roll(x, MLA_ROPE // 2, 1)
    swapped = jnp.where((lane % MLA_ROPE) < MLA_ROPE // 2, upper, lower)
    return x * cos + swapped * sin_signed


MLA_QK = MLA_KV_RANK + MLA_ROPE


def _mla_proj_kernel(merged, x_ref, mod_ref, wdown_ref, gq_ref, gkv_ref, wuq_ref, wuk_ref, cos_ref, sin_ref,
                     *out_refs):
    if merged:
        q_ref, ckv_ref, kr_ref, kcat_ref = out_refs
        qa_dst = lambda hd: q_ref.at[hd, :, :MLA_KV_RANK]
        qr_dst = lambda hd: q_ref.at[hd, :, MLA_KV_RANK:]
        q_dtype = q_ref.dtype
    else:
        qa_ref, qr_ref, ckv_ref, kr_ref = out_refs
        qa_dst = lambda hd: qa_ref.at[hd]
        qr_dst = lambda hd: qr_ref.at[hd]
        q_dtype = qa_ref.dtype
    x = x_ref[...]
    h = _modulated_rows(x, mod_ref, 0, 1)
    d = _dot(h, wdown_ref[...])
    q_lat = _rms_norm(d[:, :MLA_Q_RANK], gq_ref[...])
    ckv = _rms_norm(d[:, MLA_Q_RANK:MLA_Q_RANK + MLA_KV_RANK], gkv_ref[...])
    cos = cos_ref[...]
    sin = sin_ref[...]
    kr = _rope_tile(d[:, MLA_Q_RANK + MLA_KV_RANK:], cos, sin)[:, :MLA_ROPE]
    ckv_ref[...] = ckv
    kr_ref[...] = kr
    if merged:
        kcat_ref[:, :MLA_KV_RANK] = ckv.astype(BF16)
        kcat_ref[:, MLA_KV_RANK:] = kr.astype(BF16)
    q = _dot(q_lat.astype(BF16), wuq_ref[...])
    n_nope = MLA_HEADS * MLA_NOPE
    for hd in range(MLA_HEADS):
        qn = q[:, hd * MLA_NOPE:(hd + 1) * MLA_NOPE].astype(BF16)
        qa_dst(hd)[...] = (_dot(qn, wuk_ref[hd]) * MLA_SCALE).astype(q_dtype)
    for t in range(MLA_HEADS * MLA_ROPE // LANES):
        rot = _rope_tile(q[:, n_nope + t * LANES:n_nope + (t + 1) * LANES], cos, sin) * MLA_SCALE
        qr_dst(2 * t)[...] = rot[:, :MLA_ROPE].astype(q_dtype)
        qr_dst(2 * t + 1)[...] = rot[:, MLA_ROPE:].astype(q_dtype)


def _mla_proj_call(x, mod, layer, j, w_down, g_q, g_kv, w_uq, w_uk, cos, sin, tile_rows, merged, pos_tiles):
    s = _Stream(x, tile_rows)
    n = s.n_tokens
    if pos_tiles == 1:
        pos_spec = pl.BlockSpec((s.rows, LANES), lambda i: (0, 0))
    else:
        pos_spec = pl.BlockSpec((s.rows, LANES), lambda i: (i % pos_tiles, 0))
    if merged:
        out_specs = [pl.BlockSpec((MLA_HEADS, s.rows, MLA_QK), lambda i: (0, i, 0))]
        out_shape = [jax.ShapeDtypeStruct((MLA_HEADS, n, MLA_QK), BF16)]
    else:
        out_specs = [pl.BlockSpec((MLA_HEADS, s.rows, MLA_KV_RANK), lambda i: (0, i, 0)),
                     pl.BlockSpec((MLA_HEADS, s.rows, MLA_ROPE), lambda i: (0, i, 0))]
        out_shape = [jax.ShapeDtypeStruct((MLA_HEADS, n, MLA_KV_RANK), F32),
                     jax.ShapeDtypeStruct((MLA_HEADS, n, MLA_ROPE), F32)]
    out_specs += [s.rows_spec(MLA_KV_RANK), s.rows_spec(MLA_ROPE)]
    out_shape += [jax.ShapeDtypeStruct((n, MLA_KV_RANK), F32), jax.ShapeDtypeStruct((n, MLA_ROPE), F32)]
    if merged:
        out_specs.append(s.rows_spec(MLA_QK))
        out_shape.append(jax.ShapeDtypeStruct((n, MLA_QK), BF16))
    return pl.pallas_call(
        functools.partial(_mla_proj_kernel, merged),
        grid=(s.n_tiles,),
        in_specs=[s.x_spec(), s.mod_spec(layer), _layer_spec(w_down, j), _layer_spec(g_q, j),
                  _layer_spec(g_kv, j), _layer_spec(w_uq, j), _layer_spec(w_uk, j), pos_spec, pos_spec],
        out_specs=out_specs,
        out_shape=out_shape,
        compiler_params=_params("arbitrary"),
        name=f"mla_proj_l{layer}_g{s.g}",
    )(x, mod, w_down, g_q, g_kv, w_uq, w_uk, cos, sin)


ATTN_Q_TILE = 512
ATTN_K_TILE = 512
ATTN_ROW_PARTS = 8


def _attn_p_kernel(q_ref, k_ref, x_ref, mod_ref, wuv_ref, wo_ref, g_ref, b_ref, o_ref,
                   m_scr, l_scr, acc_scr, s_scr):
    i = pl.program_id(1)
    heads, tq, _ = q_ref.shape
    rows = heads * tq
    tk = ATTN_K_TILE
    m_scr[...] = jnp.full(m_scr.shape, NEG_BIG, F32)
    l_scr[...] = jnp.zeros(l_scr.shape, F32)
    acc_scr[...] = jnp.zeros(acc_scr.shape, F32)
    part_rows = rows // ATTN_ROW_PARTS
    parts = [slice(p * part_rows, (p + 1) * part_rows) for p in range(ATTN_ROW_PARTS)]

    def q_part(rs):
        return q_ref[rs.start // tq:rs.stop // tq].reshape(part_rows, MLA_QK)

    def keys(jb):
        return k_ref[pl.ds(pl.multiple_of(jb * tk, tk), tk), :]

    def absorb(rs, s, vb, masked):
        if masked:
            q_pos = lax.broadcasted_iota(jnp.int32, s.shape, 0) % tq
            k_pos = lax.broadcasted_iota(jnp.int32, s.shape, 1)
            s = jnp.where(k_pos <= q_pos, s, NEG_BIG)
        m_prev = m_scr[rs, :]
        m_next = jnp.maximum(m_prev, jnp.max(s, axis=1, keepdims=True))
        p = jnp.exp(s - jnp.concatenate([m_next] * (tk // LANES), axis=1))
        alpha = jnp.exp(m_prev - m_next)
        l_scr[rs, :] = alpha * l_scr[rs, :] + jnp.sum(p, axis=1, keepdims=True)
        m_scr[rs, :] = m_next
        acc_scr[rs, :] = (acc_scr[rs, :] * jnp.concatenate([alpha] * (MLA_KV_RANK // LANES), axis=1)
                          + _dot(p.astype(BF16), vb))

    kb0 = keys(0)
    for rs in parts:
        s_scr[rs, :] = _dot_nt(q_part(rs), kb0)

    def body(jb, carry):
        vb = keys(jb)[:, :MLA_KV_RANK]
        kb_next = keys(jb + 1)
        for rs in parts:
            s = s_scr[rs, :]
            s_scr[rs, :] = _dot_nt(q_part(rs), kb_next)
            absorb(rs, s, vb, False)
        return carry

    lax.fori_loop(0, i, body, 0)
    vb = keys(i)[:, :MLA_KV_RANK]
    for rs in parts:
        absorb(rs, s_scr[rs, :], vb, True)
    o_heads = []
    for hd in range(heads):
        rs = slice(hd * tq, (hd + 1) * tq)
        inv = 1.0 / l_scr[rs, :]
        ctx = (acc_scr[rs, :] * jnp.concatenate([inv] * (MLA_KV_RANK // LANES), axis=1)).astype(BF16)
        o_heads.append(_dot(ctx, wuv_ref[hd]).astype(BF16))
    y = _dot(jnp.concatenate(o_heads, axis=1), wo_ref[...])
    o_ref[...] = _residual_norm(x_ref[...], y, _mod(mod_ref, 2), g_ref, b_ref)


def _attn_p_call(q, kcat, x, mod, layer, j, w_uv, w_o, mix_g, mix_b):
    assert ATTN_Q_TILE == ATTN_K_TILE
    n_batch, seq, _ = x.shape
    nq = seq // ATTN_Q_TILE
    rows = MLA_HEADS * ATTN_Q_TILE
    xspec = pl.BlockSpec((1, ATTN_Q_TILE, D_MODEL), lambda b, i: (b, i, 0))
    return pl.pallas_call(
        _attn_p_kernel,
        grid=(n_batch, nq),
        in_specs=[pl.BlockSpec((MLA_HEADS, ATTN_Q_TILE, MLA_QK), lambda b, i: (0, b * nq + i, 0)),
                  pl.BlockSpec((seq, MLA_QK), lambda b, i: (b, 0)),
                  xspec,
                  pl.BlockSpec((None, 1, 1, 6 * D_MODEL), lambda b, i: (layer, b, 0, 0)),
                  _layer_spec(w_uv, j), _layer_spec(w_o, j), _layer_spec(mix_g, layer),
                  _layer_spec(mix_b, layer)],
        out_specs=xspec,
        out_shape=jax.ShapeDtypeStruct(x.shape, F32),
        scratch_shapes=[pltpu.VMEM((rows, LANES), F32), pltpu.VMEM((rows, LANES), F32),
                        pltpu.VMEM((rows, MLA_KV_RANK), F32), pltpu.VMEM((rows, ATTN_K_TILE), F32)],
        compiler_params=_params("arbitrary", "arbitrary"),
        name="mla_attn_p",
    )(q, kcat, x, mod, w_uv, w_o, mix_g, mix_b)


ATTN_S_SPLITS = 4


def _attn_s_kernel(j, pt_ref, qa_ref, qr_ref, kn_ref, rn_ref, cache_k, cache_rt, o_ref, kbuf, rbuf, sems):
    b = pl.program_id(0)
    nb = pl.num_programs(0)
    n_pages = pt_ref.shape[1]
    page = cache_k.shape[2]
    slot = b % 2

    def copies(pg, sl, p):
        keys = pl.ds(p * page, page)
        return (pltpu.make_async_copy(cache_k.at[j, pg], kbuf.at[sl, keys, :], sems.at[0, sl]),
                pltpu.make_async_copy(cache_rt.at[j, pg], rbuf.at[sl, :, keys], sems.at[1, sl]))

    def start_all(bi, sl):
        for p in range(n_pages):
            for c in copies(pt_ref[bi, p], sl, p):
                c.start(priority=p % 2)

    def wait_all(sl):
        for p in range(n_pages):
            for c in copies(0, sl, p):
                c.wait()

    @pl.when(b == 0)
    def _():
        start_all(0, 0)

    @pl.when(b + 1 < nb)
    def _():
        start_all(b + 1, 1 - slot)

    wait_all(slot)

    heads, t, _ = qa_ref.shape
    rows = heads * t
    qa = qa_ref[...].reshape(rows, MLA_KV_RANK).astype(BF16)
    qr = qr_ref[...].reshape(rows, MLA_ROPE).astype(BF16)
    kn = kn_ref[...].astype(BF16)
    rn = rn_ref[...].astype(BF16)

    s_new = _dot_nt(qa, kn) + _dot_nt(qr, rn)
    q_pos = lax.broadcasted_iota(jnp.int32, s_new.shape, 0) % t
    k_pos = lax.broadcasted_iota(jnp.int32, s_new.shape, 1)
    s_new = jnp.where(k_pos <= q_pos, s_new, NEG_BIG)
    m_new = jnp.max(s_new, axis=1, keepdims=True)
    p_new = jnp.exp(s_new - m_new)
    parts = [(m_new, jnp.sum(p_new, axis=1, keepdims=True), _dot(p_new.astype(BF16), kn))]

    span = n_pages * page // ATTN_S_SPLITS
    chunks = [slice(c * span, (c + 1) * span) for c in range(ATTN_S_SPLITS)]
    kps = [kbuf[slot, keys, :].astype(BF16) for keys in chunks]
    ss = [_dot_nt(qa, kp) + _dot(qr, rbuf[slot, :, keys].astype(BF16)) for kp, keys in zip(kps, chunks)]
    ms = [jnp.max(s, axis=1, keepdims=True) for s in ss]
    ps = [jnp.exp(s - m_c) for s, m_c in zip(ss, ms)]
    for m_c, p, kp in zip(ms, ps, kps):
        parts.append((m_c, jnp.sum(p, axis=1, keepdims=True), _dot(p.astype(BF16), kp)))

    m = functools.reduce(jnp.maximum, [pt[0] for pt in parts])
    scales = [jnp.exp(pt[0] - m) for pt in parts]
    denom = sum(sc * pt[1] for sc, pt in zip(scales, parts))
    ctx = sum(sc * pt[2] for sc, pt in zip(scales, parts)) / denom
    for hd in range(heads):
        o_ref[:, hd * MLA_KV_RANK:(hd + 1) * MLA_KV_RANK] = ctx[hd * t:(hd + 1) * t]


def _attn_s_call(page_table, qa, qr, ckv_new, kr_new, cache_k, cache_r, j, n_batch, t):
    past = page_table.shape[1] * cache_k.shape[2]
    grid_spec = pltpu.PrefetchScalarGridSpec(
        num_scalar_prefetch=1,
        grid=(n_batch,),
        in_specs=[pl.BlockSpec((MLA_HEADS, t, MLA_KV_RANK), lambda b, pt: (0, b, 0)),
                  pl.BlockSpec((MLA_HEADS, t, MLA_ROPE), lambda b, pt: (0, b, 0)),
                  pl.BlockSpec((t, MLA_KV_RANK), lambda b, pt: (b, 0)),
                  pl.BlockSpec((t, MLA_ROPE), lambda b, pt: (b, 0)),
                  pl.BlockSpec(memory_space=pl.ANY),
                  pl.BlockSpec(memory_space=pl.ANY)],
        out_specs=pl.BlockSpec((t, MLA_HEADS * MLA_KV_RANK), lambda b, pt: (b, 0)),
        scratch_shapes=[pltpu.VMEM((2, past, MLA_KV_RANK), F32),
                        pltpu.VMEM((2, MLA_ROPE, past), F32),
                        pltpu.SemaphoreType.DMA((2, 2))],
    )
    return pl.pallas_call(
        functools.partial(_attn_s_kernel, j),
        grid_spec=grid_spec,
        out_shape=jax.ShapeDtypeStruct((n_batch * t, MLA_HEADS * MLA_KV_RANK), F32),
        compiler_params=_params("arbitrary"),
        name="mla_attn_s",
    )(page_table, qa, qr, ckv_new, kr_new, cache_k, cache_r)


def _mla_out_kernel(x_ref, mod_ref, ctx_ref, wuv_ref, wo_ref, g_ref, b_ref, o_ref):
    x = x_ref[...]
    parts = []
    for hd in range(MLA_HEADS):
        c = ctx_ref[:, hd * MLA_KV_RANK:(hd + 1) * MLA_KV_RANK].astype(BF16)
        parts.append(_dot(c, wuv_ref[hd]).astype(BF16))
    o = jnp.concatenate(parts, axis=1)
    y = _dot(o, wo_ref[...])
    o_ref[...] = _residual_norm(x, y, _mod(mod_ref, 2), g_ref, b_ref)


def _mla_out_call(x, mod, layer, j, ctx, w_uv, w_o, mix_g, mix_b, tile_rows):
    s = _Stream(x, tile_rows)
    return pl.pallas_call(
        _mla_out_kernel,
        grid=(s.n_tiles,),
        in_specs=[s.x_spec(), s.mod_spec(layer), s.rows_spec(MLA_HEADS * MLA_KV_RANK),
                  _layer_spec(w_uv, j), _layer_spec(w_o, j), _layer_spec(mix_g, layer),
                  _layer_spec(mix_b, layer)],
        out_specs=s.x_spec(),
        out_shape=s.x_shape(),
        compiler_params=_params("arbitrary"),
        name=f"mla_out_l{layer}_g{s.g}",
    )(x, mod, ctx, w_uv, w_o, mix_g, mix_b)


def _row3(a):
    return a.reshape(a.shape[0], 1, a.shape[1])


def _rope_tables(pos):
    half = MLA_ROPE // 2
    inv = ROPE_THETA ** (-jnp.arange(half, dtype=F32) / half)
    ang = pos.astype(F32)[:, None] * inv[None, :]
    cos, sin = jnp.cos(ang), jnp.sin(ang)
    cos64 = jnp.concatenate([cos, cos], -1)
    sin64 = jnp.concatenate([-sin, sin], -1)
    return jnp.tile(cos64, (1, LANES // MLA_ROPE)), jnp.tile(sin64, (1, LANES // MLA_ROPE))


def kernel(x_prompt, x_sample, cache_ckv, cache_krope, state_conv, page_table, c_prompt, c_sample,
           ada_w, ada_b, ln_mix_g, ln_mix_b, ln_ffn_g, ln_ffn_b, ffn_w_gate, ffn_w_up, ffn_w_down,
           gmlp_w_in, gmlp_ln_g, gmlp_ln_b, gmlp_w_s, gmlp_b_s, gmlp_w_out,
           conv_w_pw1, conv_b_pw1, conv_w_dw, conv_b_dw, conv_ln_g, conv_ln_b, conv_w_pw2, conv_b_pw2,
           mla_w_down, mla_g_q, mla_g_kv, mla_w_uq, mla_w_uk, mla_w_uv, mla_w_o):
    n_b, seq, _ = x_prompt.shape
    n_db, dec_seq, _ = x_sample.shape
    past_len = page_table.shape[1] * cache_ckv.shape[2]

    c_p = jnp.pad(c_prompt, ((0, SUBLANES - n_b), (0, 0)))
    mod_p, mod_s = _ada_call(c_p, c_sample, ada_w, ada_b)
    mod_p = mod_p.reshape(DEPTH, SUBLANES, 1, 6 * D_MODEL)
    mod_s = mod_s.reshape(DEPTH, n_db, 1, 6 * D_MODEL)

    ln_mix_g, ln_mix_b, ln_ffn_g, ln_ffn_b = map(_row3, (ln_mix_g, ln_mix_b, ln_ffn_g, ln_ffn_b))
    wg, wu, wd = (w.astype(BF16) for w in (ffn_w_gate, ffn_w_up, ffn_w_down))

    g_w_in, g_w_out = gmlp_w_in.astype(BF16), gmlp_w_out.astype(BF16)
    g_ln_g, g_ln_b = _row3(gmlp_ln_g), _row3(gmlp_ln_b)
    reps = GMLP_CHUNK // dec_seq
    eye = jnp.eye(reps, dtype=F32)

    c_w_pw1, c_w_pw2 = conv_w_pw1.astype(BF16), conv_w_pw2.astype(BF16)
    c_b_pw1, c_b_dw, c_ln_g, c_ln_b, c_b_pw2 = map(_row3, (conv_b_pw1, conv_b_dw, conv_ln_g, conv_ln_b,
                                                            conv_b_pw2))
    c_w_dw8 = jnp.broadcast_to(conv_w_dw[:, :, None, :], conv_w_dw.shape[:2] + (SUBLANES, D_MODEL))
    tap = jnp.arange(CONV_CTX + dec_seq)[:, None] - jnp.arange(dec_seq)[None, :]
    tap_ok = (tap >= 0) & (tap < CONV_WIDTH)
    conv_coef = jnp.where(tap_ok[None, :, :, None], conv_w_dw[:, jnp.clip(tap, 0, CONV_WIDTH - 1)], 0.0)

    n_mla = mla_w_down.shape[0]
    m_w_down = jnp.pad(mla_w_down, ((0, 0), (0, 0), (0, MLA_DOWN_PAD - mla_w_down.shape[-1]))).astype(BF16)
    w_uq = mla_w_uq.reshape(n_mla, MLA_Q_RANK, MLA_HEADS, MLA_NOPE + MLA_ROPE)
    m_w_uq = jnp.concatenate([w_uq[..., :MLA_NOPE].reshape(n_mla, MLA_Q_RANK, -1),
                              w_uq[..., MLA_NOPE:].reshape(n_mla, MLA_Q_RANK, -1)], -1).astype(BF16)
    m_w_uk = mla_w_uk.transpose(0, 2, 3, 1).astype(BF16)
    m_w_uv = mla_w_uv.transpose(0, 2, 1, 3).astype(BF16)
    m_w_o = mla_w_o.astype(BF16)
    m_g_q, m_g_kv = _row3(mla_g_q), _row3(mla_g_kv)
    cos_p, sin_p = _rope_tables(jnp.arange(seq))
    cos_s, sin_s = _rope_tables(past_len + jnp.arange(dec_seq))

    ffn_tile = 512
    gm_tile = 512
    conv_tile = 512
    mla_tile = 512
    s_reps = mla_tile // dec_seq
    cos_s, sin_s = jnp.tile(cos_s, (s_reps, 1)), jnp.tile(sin_s, (s_reps, 1))

    xp, xs = x_prompt, x_sample
    ckv_p_rows, kr_p_rows, ckv_s_rows, kr_s_rows = [], [], [], []
    conv_p_states, conv_s_states, gmlp_v_rows = [], [], []
    for i in range(DEPTH):
        kind, j = i % N_MIXERS, i // N_MIXERS
        if kind == 0:
            wmix_p = gmlp_w_s[j].astype(BF16)
            bmix_p = gmlp_b_s[j].T
            corner = gmlp_w_s[j, :, :dec_seq, :dec_seq]
            wmix_s = jnp.einsum('ab,gts->gatbs', eye, corner).reshape(
                GMLP_GROUPS, GMLP_CHUNK, GMLP_CHUNK).astype(BF16)
            bmix_s = jnp.tile(gmlp_b_s[j, :, :dec_seq].T, (reps, 1))
            xp, xs, v_s = _gmlp_call(xp, mod_p, xs, mod_s, i, j, g_w_in, g_ln_g, g_ln_b, wmix_p, bmix_p,
                                     wmix_s, bmix_s, g_w_out, ln_mix_g, ln_mix_b, gm_tile)
            gmlp_v_rows.append(v_s)
        elif kind == 1:
            xp, st_p = _conv_p_call(xp, mod_p, i, j, c_w_pw1, c_b_pw1, c_w_dw8, c_b_dw, c_ln_g, c_ln_b,
                                    c_w_pw2, c_b_pw2, ln_mix_g, ln_mix_b, conv_tile)
            xs_new, a_s = _conv_s_call(xs, mod_s, i, j, state_conv, c_w_pw1, c_b_pw1, conv_coef[j], c_b_dw,
                                       c_ln_g, c_ln_b, c_w_pw2, c_b_pw2, ln_mix_g, ln_mix_b, conv_tile)
            xs = xs_new
            conv_p_states.append(st_p)
            conv_s_states.append(jnp.concatenate([state_conv[j][:, dec_seq:], a_s], axis=1))
        else:
            q_p, ckv, kr, kcat = _mla_proj_call(
                xp, mod_p, i, j, m_w_down, m_g_q, m_g_kv, m_w_uq, m_w_uk, cos_p, sin_p, mla_tile, True,
                seq // mla_tile)
            xp = _attn_p_call(q_p, kcat, xp, mod_p, i, j, m_w_uv, m_w_o, ln_mix_g, ln_mix_b)
            ckv_p_rows.append(ckv.reshape(n_b, seq, MLA_KV_RANK))
            kr_p_rows.append(kr.reshape(n_b, seq, MLA_ROPE))
            qa, qr, ckv, kr = _mla_proj_call(
                xs, mod_s, i, j, m_w_down, m_g_q, m_g_kv, m_w_uq, m_w_uk, cos_s, sin_s, mla_tile, False, 1)
            ctx_s = _attn_s_call(page_table, qa, qr, ckv, kr, cache_ckv, jnp.swapaxes(cache_krope, 2, 3), j,
                                 n_db, dec_seq)
            xs = _mla_out_call(xs, mod_s, i, j, ctx_s, m_w_uv, m_w_o, ln_mix_g, ln_mix_b, mla_tile)
            ckv_s_rows.append(ckv.reshape(n_db, dec_seq, MLA_KV_RANK))
            kr_s_rows.append(kr.reshape(n_db, dec_seq, MLA_ROPE))
        xp, xs = _ffn_call(xp, mod_p, xs, mod_s, i, wg, wu, wd, ln_ffn_g, ln_ffn_b, ffn_tile)
    return (xp, xs, jnp.stack(ckv_p_rows), jnp.stack(kr_p_rows), jnp.stack(ckv_s_rows), jnp.stack(kr_s_rows),
            jnp.stack(conv_p_states), jnp.stack(conv_s_states), jnp.stack(gmlp_v_rows))
```

```python
import functools
import math

import jax
import jax.numpy as jnp
from jax import lax
from jax.experimental import pallas as pl
from jax.experimental.pallas import tpu as pltpu

F32 = jnp.float32
BF16 = jnp.bfloat16

D_MODEL = 1024
DEPTH = 4
N_MIXERS = 3
ALPHA = (2.0 * DEPTH) ** 0.25
LN_EPS = 1e-5
RMS_EPS = 1e-6

GMLP_DIM = 2 * D_MODEL
GMLP_GROUPS = 8
GMLP_CHUNK = 128
GMLP_GROUP_WIDTH = GMLP_DIM // GMLP_GROUPS

CONV_WIDTH = 31
CONV_CTX = CONV_WIDTH - 1

MLA_HEADS = 8
MLA_Q_RANK = 384
MLA_KV_RANK = 256
MLA_NOPE = 128
MLA_ROPE = 64
MLA_V = 128
ROPE_THETA = 10000.0
MLA_SCALE = (MLA_NOPE + MLA_ROPE) ** -0.5
MLA_DOWN_PAD = MLA_Q_RANK + MLA_KV_RANK + 128

FFN_DIM = ((8 * D_MODEL + 3 * 256 - 1) // (3 * 256)) * 256

VMEM_LIMIT_BYTES = 56 * 1024 * 1024
LANES = 128
SUBLANES = 8

NEG_BIG = -1e30


def _params(*semantics):
    return pltpu.CompilerParams(dimension_semantics=semantics, vmem_limit_bytes=VMEM_LIMIT_BYTES)


def _dot(a, b):
    return jnp.dot(a, b, preferred_element_type=F32)


def _dot_nt(a, b):
    return lax.dot_general(a, b, (((1,), (1,)), ((), ())), preferred_element_type=F32)


def _layer_norm(x, g, b):
    mu = jnp.mean(x, -1, keepdims=True)
    xc = x - mu
    var = jnp.mean(xc * xc, -1, keepdims=True)
    return xc * lax.rsqrt(var + LN_EPS) * g + b


def _rms_norm(x, g):
    return x * lax.rsqrt(jnp.mean(x * x, -1, keepdims=True) + RMS_EPS) * g


def _mod(mod_ref, k, groups=slice(None)):
    return mod_ref[groups, :, k * D_MODEL:(k + 1) * D_MODEL]


def _modulated_rows(x, mod_ref, k_shift, k_scale, groups=slice(None)):
    g, r, d = x.shape
    h = x * (1.0 + _mod(mod_ref, k_scale, groups)) + _mod(mod_ref, k_shift, groups)
    return h.reshape(g * r, d).astype(BF16)


def _row_parts(x_ref, n):
    g, r, _ = x_ref.shape
    if g == 1:
        return [(slice(0, 1), slice(p * (r // n), (p + 1) * (r // n))) for p in range(n)]
    return [(slice(p * (g // n), (p + 1) * (g // n)), slice(0, r)) for p in range(n)]


def _residual_norm(x, y_rows, gate, g_ref, b_ref):
    return _layer_norm(ALPHA * x + gate * y_rows.reshape(x.shape), g_ref[...], b_ref[...])


def _ada_kernel(cp_ref, cs_ref, w_ref, b_ref, mp_ref, ms_ref):
    w = w_ref[...].astype(BF16)
    bias = b_ref[...]
    for c_ref, o_ref in ((cp_ref, mp_ref), (cs_ref, ms_ref)):
        c = c_ref[...]
        a = (c * jax.nn.sigmoid(c)).astype(BF16)
        o_ref[...] = _dot(a, w) + bias


def _ada_call(c_p, c_s, ada_w, ada_b):
    tn = 1536
    n_p, n_s = c_p.shape[0], c_s.shape[0]
    d6 = ada_w.shape[-1]
    return pl.pallas_call(
        _ada_kernel,
        grid=(DEPTH, d6 // tn),
        in_specs=[
            pl.BlockSpec((n_p, D_MODEL), lambda l, n: (0, 0)),
            pl.BlockSpec((n_s, D_MODEL), lambda l, n: (0, 0)),
            pl.BlockSpec((None, D_MODEL, tn), lambda l, n: (l, 0, n)),
            pl.BlockSpec((None, 1, tn), lambda l, n: (l, 0, n)),
        ],
        out_specs=[
            pl.BlockSpec((None, n_p, tn), lambda l, n: (l, 0, n)),
            pl.BlockSpec((None, n_s, tn), lambda l, n: (l, 0, n)),
        ],
        out_shape=[
            jax.ShapeDtypeStruct((DEPTH, n_p, d6), F32),
            jax.ShapeDtypeStruct((DEPTH, n_s, d6), F32),
        ],
        compiler_params=_params("arbitrary", "arbitrary"),
        name="ada_table",
    )(c_p, c_s, ada_w, ada_b.reshape(DEPTH, 1, d6))


class _Stream:
    def __init__(self, x, tile_rows, first=None):
        self.first = first
        self.n_groups, self.group_rows, _ = x.shape
        if self.group_rows >= tile_rows:
            assert self.group_rows % tile_rows == 0
            self.g, self.r = 1, tile_rows
            self.tiles_per_group = self.group_rows // tile_rows
        else:
            assert tile_rows % self.group_rows == 0
            self.g, self.r = tile_rows // self.group_rows, self.group_rows
            assert self.n_groups % self.g == 0
            self.tiles_per_group = 1
        self.rows = self.g * self.r
        self.n_tiles = (self.n_groups // self.g) * self.tiles_per_group
        self.n_tokens = self.n_groups * self.group_rows

    def _tile(self, i):
        return i if self.first is None else jnp.clip(i - self.first, 0, self.n_tiles - 1)

    def _gi(self, i):
        t = self._tile(i)
        if self.tiles_per_group == 1:
            return t, 0
        return t // self.tiles_per_group, t % self.tiles_per_group

    def x_spec(self, width=D_MODEL):
        return pl.BlockSpec((self.g, self.r, width), lambda i: (*self._gi(i), 0))

    def mod_spec(self, layer):
        return pl.BlockSpec((None, self.g, 1, 6 * D_MODEL), lambda i: (layer, self._gi(i)[0], 0, 0))

    def rows_spec(self, width):
        return pl.BlockSpec((self.rows, width), lambda i: (self._tile(i), 0))

    def x_shape(self, width=D_MODEL, dtype=F32):
        return jax.ShapeDtypeStruct((self.n_groups, self.group_rows, width), dtype)


def _const_spec(shape):
    zeros = (0,) * len(shape)
    return pl.BlockSpec(shape, lambda *_: zeros, pipeline_mode=pl.Buffered(1))


def _layer_spec(arr, layer):
    zeros = (0,) * (arr.ndim - 1)
    return pl.BlockSpec((None,) + arr.shape[1:], lambda *_: (layer,) + zeros,
                        pipeline_mode=pl.Buffered(1))


MXU_WIDTH = 256
FFN_SPLIT = (FFN_DIM // MXU_WIDTH + 1) // 2 * MXU_WIDTH


FFN_ROW_PARTS = 2
GMLP_ROW_PARTS = 2
CONV_ROW_PARTS = 2


def _ffn_tile(x_ref, mod_ref, wg_ref, wu_ref, wd_ref, g_ref, b_ref, o_ref, a_scr):
    parts = _row_parts(x_ref, FFN_ROW_PARTS)
    xs = [x_ref[gs, rs, :] for gs, rs in parts]
    hs = [_modulated_rows(x, mod_ref, 3, 4, gs) for x, (gs, _) in zip(xs, parts)]
    fs = []
    r0 = 0
    for h in hs:
        rows = h.shape[0]
        for sl in (slice(0, FFN_SPLIT), slice(FFN_SPLIT, FFN_DIM)):
            gate = _dot(h, wg_ref[:, sl])
            up = _dot(h, wu_ref[:, sl])
            a_scr[r0:r0 + rows, sl] = (gate * jax.nn.sigmoid(gate) * up).astype(BF16)
        fs.append(_dot(a_scr[r0:r0 + rows, :], wd_ref[...]))
        r0 += rows
    for x, f, (gs, rs) in zip(xs, fs, parts):
        o_ref[gs, rs, :] = _residual_norm(x, f, _mod(mod_ref, 5, gs), g_ref, b_ref)


def _ffn_kernel(n_p, xp_ref, modp_ref, xs_ref, mods_ref, wg_ref, wu_ref, wd_ref, g_ref, b_ref,
                op_ref, os_ref, a_scr):
    i = pl.program_id(0)

    @pl.when(i < n_p)
    def _():
        _ffn_tile(xp_ref, modp_ref, wg_ref, wu_ref, wd_ref, g_ref, b_ref, op_ref, a_scr)

    @pl.when(i >= n_p)
    def _():
        _ffn_tile(xs_ref, mods_ref, wg_ref, wu_ref, wd_ref, g_ref, b_ref, os_ref, a_scr)


def _ffn_call(xp, mod_p, xs, mod_s, layer, wg, wu, wd, ln_g, ln_b, tile_rows):
    sp = _Stream(xp, tile_rows, first=0)
    ss = _Stream(xs, tile_rows, first=sp.n_tiles)
    assert sp.rows == ss.rows
    return pl.pallas_call(
        functools.partial(_ffn_kernel, sp.n_tiles),
        grid=(sp.n_tiles + ss.n_tiles,),
        in_specs=[sp.x_spec(), sp.mod_spec(layer), ss.x_spec(), ss.mod_spec(layer),
                  _layer_spec(wg, layer), _layer_spec(wu, layer), _layer_spec(wd, layer),
                  _layer_spec(ln_g, layer), _layer_spec(ln_b, layer)],
        out_specs=[sp.x_spec(), ss.x_spec()],
        out_shape=[sp.x_shape(), ss.x_shape()],
        scratch_shapes=[pltpu.VMEM((sp.rows, FFN_DIM), BF16)],
        compiler_params=_params("arbitrary"),
        name=f"ffn_l{layer}",
    )(xp, mod_p, xs, mod_s, wg, wu, wd, ln_g, ln_b)


def _gelu(z):
    return 0.5 * z * (1.0 + lax.erf(z * (1.0 / math.sqrt(2.0))))


def _gmlp_tile(x_ref, mod_ref, win_ref, lng_ref, lnb_ref, wmix_ref, bmix_ref, wout_ref, g_ref, b_ref,
               o_ref, v_ref, um_scr):
    row_id = lax.broadcasted_iota(jnp.int32, (GMLP_CHUNK, GMLP_CHUNK), 0)
    col_id = lax.broadcasted_iota(jnp.int32, (GMLP_CHUNK, GMLP_CHUNK), 1)
    causal = row_id >= col_id
    parts = _row_parts(x_ref, GMLP_ROW_PARTS)
    xs = [x_ref[gs, rs, :] for gs, rs in parts]
    zs = [_dot(_modulated_rows(x, mod_ref, 0, 1, gs), win_ref[...]) for x, (gs, _) in zip(xs, parts)]
    ys = []
    r0 = 0
    for x, z, (gs, rs) in zip(xs, zs, parts):
        rows = z.shape[0]
        z = _gelu(z)
        u = z[:, :GMLP_DIM]
        v = _layer_norm(z[:, GMLP_DIM:], lng_ref[...], lnb_ref[...])
        if v_ref is not None:
            v_ref[gs, rs, :] = v.reshape(x.shape[:2] + (GMLP_DIM,))
        vb = v.astype(BF16)
        for gi in range(GMLP_GROUPS):
            w = jnp.where(causal, wmix_ref[gi], jnp.zeros((), BF16))
            bias = bmix_ref[:, gi:gi + 1]
            cols = slice(gi * GMLP_GROUP_WIDTH, (gi + 1) * GMLP_GROUP_WIDTH)
            for c in range(rows // GMLP_CHUNK):
                rws = slice(c * GMLP_CHUNK, (c + 1) * GMLP_CHUNK)
                mixed = _dot(w, vb[rws, cols]) + bias
                um_scr[r0 + c * GMLP_CHUNK:r0 + (c + 1) * GMLP_CHUNK, cols] = (u[rws, cols] * mixed).astype(BF16)
        ys.append(_dot(um_scr[r0:r0 + rows, :], wout_ref[...]))
        r0 += rows
    for x, y, (gs, rs) in zip(xs, ys, parts):
        o_ref[gs, rs, :] = _residual_norm(x, y, _mod(mod_ref, 2, gs), g_ref, b_ref)


def _gmlp_kernel(n_p, xp_ref, modp_ref, xs_ref, mods_ref, win_ref, lng_ref, lnb_ref, wmixp_ref, bmixp_ref,
                 wmixs_ref, bmixs_ref, wout_ref, g_ref, b_ref, op_ref, os_ref, v_ref, um_scr):
    i = pl.program_id(0)

    @pl.when(i < n_p)
    def _():
        _gmlp_tile(xp_ref, modp_ref, win_ref, lng_ref, lnb_ref, wmixp_ref, bmixp_ref, wout_ref, g_ref, b_ref,
                   op_ref, None, um_scr)

    @pl.when(i >= n_p)
    def _():
        _gmlp_tile(xs_ref, mods_ref, win_ref, lng_ref, lnb_ref, wmixs_ref, bmixs_ref, wout_ref, g_ref, b_ref,
                   os_ref, v_ref, um_scr)


def _gmlp_call(xp, mod_p, xs, mod_s, layer, j, w_in, ln_g, ln_b, wmix_p, bmix_p, wmix_s, bmix_s, w_out,
               mix_g, mix_b, tile_rows):
    sp = _Stream(xp, tile_rows, first=0)
    ss = _Stream(xs, tile_rows, first=sp.n_tiles)
    assert sp.rows == ss.rows
    return pl.pallas_call(
        functools.partial(_gmlp_kernel, sp.n_tiles),
        grid=(sp.n_tiles + ss.n_tiles,),
        in_specs=[sp.x_spec(), sp.mod_spec(layer), ss.x_spec(), ss.mod_spec(layer),
                  _layer_spec(w_in, j), _layer_spec(ln_g, j), _layer_spec(ln_b, j),
                  _const_spec(wmix_p.shape), _const_spec(bmix_p.shape),
                  _const_spec(wmix_s.shape), _const_spec(bmix_s.shape),
                  _layer_spec(w_out, j), _layer_spec(mix_g, layer), _layer_spec(mix_b, layer)],
        out_specs=[sp.x_spec(), ss.x_spec(), ss.x_spec(GMLP_DIM)],
        out_shape=[sp.x_shape(), ss.x_shape(), ss.x_shape(GMLP_DIM)],
        scratch_shapes=[pltpu.VMEM((sp.rows, GMLP_DIM), BF16)],
        compiler_params=_params("arbitrary"),
        name=f"gmlp_l{layer}",
    )(xp, mod_p, xs, mod_s, w_in, ln_g, ln_b, wmix_p, bmix_p, wmix_s, bmix_s, w_out, mix_g, mix_b)


CONV_HALO = 32
CONV_ROW_CHUNK = 128
CONV_LANE_CHUNK = 256


def _glu_rows(h, wpw1_ref, bpw1_ref):
    a = _dot(h, wpw1_ref[...]) + bpw1_ref[...]
    return a[:, :D_MODEL] * jax.nn.sigmoid(a[:, D_MODEL:])


def _conv_tail(x, y, mod_ref, cg_ref, cb_ref, wpw2_ref, bpw2_ref, g_ref, b_ref):
    y = _layer_norm(y, cg_ref[...], cb_ref[...])
    y = (y * jax.nn.sigmoid(y)).astype(BF16)
    out = _dot(y, wpw2_ref[...]) + bpw2_ref[...]
    return _residual_norm(x, out, _mod(mod_ref, 2), g_ref, b_ref)


def _conv_p_kernel(x_ref, mod_ref, wpw1_ref, bpw1_ref, wdw_ref, bdw_ref, cg_ref, cb_ref, wpw2_ref, bpw2_ref,
                   g_ref, b_ref, o_ref, st_ref, buf_scr, y_scr):
    t = pl.program_id(1)
    rows = x_ref.shape[1]
    part_rows = rows // CONV_ROW_PARTS
    off = CONV_HALO - CONV_CTX

    @pl.when(t == 0)
    def _():
        buf_scr[0:CONV_HALO, :] = jnp.zeros((CONV_HALO, D_MODEL), F32)

    xs = [x_ref[:, p * part_rows:(p + 1) * part_rows, :] for p in range(CONV_ROW_PARTS)]
    for p, x in enumerate(xs):
        h = _modulated_rows(x, mod_ref, 0, 1)
        buf_scr[CONV_HALO + p * part_rows:CONV_HALO + (p + 1) * part_rows, :] = _glu_rows(h, wpw1_ref, bpw1_ref)

    n_q = (CONV_WIDTH + SUBLANES - 1) // SUBLANES
    window = CONV_ROW_CHUNK + CONV_HALO
    tiles = CONV_ROW_CHUNK // SUBLANES
    for p, x in enumerate(xs):
        for lc in range(D_MODEL // CONV_LANE_CHUNK):
            lanes = slice(lc * CONV_LANE_CHUNK, (lc + 1) * CONV_LANE_CHUNK)
            for rc in range(part_rows // CONV_ROW_CHUNK):
                r0 = p * part_rows + rc * CONV_ROW_CHUNK
                win = buf_scr[r0:r0 + window, lanes]
                acc = jnp.broadcast_to(bdw_ref[:, lanes][None], (tiles, SUBLANES, CONV_LANE_CHUNK))
                for r in range(SUBLANES):
                    shift = off + r
                    phase = win if shift % window == 0 else pltpu.roll(win, window - shift, 0)
                    for q in range(n_q):
                        k = SUBLANES * q + r
                        if k < CONV_WIDTH:
                            assert SUBLANES * q + CONV_ROW_CHUNK + shift <= window
                            taps = phase[SUBLANES * q:SUBLANES * q + CONV_ROW_CHUNK, :]
                            acc = acc + wdw_ref[k, :, lanes][None] * taps.reshape(tiles, SUBLANES, CONV_LANE_CHUNK)
                y_scr[r0:r0 + CONV_ROW_CHUNK, lanes] = acc.reshape(CONV_ROW_CHUNK, CONV_LANE_CHUNK)
        o_ref[:, p * part_rows:(p + 1) * part_rows, :] = _conv_tail(
            x, y_scr[p * part_rows:(p + 1) * part_rows, :], mod_ref, cg_ref, cb_ref, wpw2_ref, bpw2_ref, g_ref, b_ref)

    @pl.when(t == pl.num_programs(1) - 1)
    def _():
        st_ref[...] = buf_scr[rows + off:rows + CONV_HALO, :].reshape(st_ref.shape)

    buf_scr[0:CONV_HALO, :] = buf_scr[rows:rows + CONV_HALO, :]


def _conv_p_call(x, mod, layer, j, w_pw1, b_pw1, w_dw, b_dw, cln_g, cln_b, w_pw2, b_pw2, mix_g, mix_b,
                 tile_rows):
    n_b, seq, _ = x.shape
    n_t = seq // tile_rows
    xspec = pl.BlockSpec((1, tile_rows, D_MODEL), lambda b, t: (b, t, 0))
    return pl.pallas_call(
        _conv_p_kernel,
        grid=(n_b, n_t),
        in_specs=[xspec,
                  pl.BlockSpec((None, 1, 1, 6 * D_MODEL), lambda b, t: (layer, b, 0, 0)),
                  _layer_spec(w_pw1, j), _layer_spec(b_pw1, j), _layer_spec(w_dw, j), _layer_spec(b_dw, j),
                  _layer_spec(cln_g, j), _layer_spec(cln_b, j), _layer_spec(w_pw2, j), _layer_spec(b_pw2, j),
                  _layer_spec(mix_g, layer), _layer_spec(mix_b, layer)],
        out_specs=[xspec, pl.BlockSpec((1, CONV_CTX, D_MODEL), lambda b, t: (b, 0, 0))],
        out_shape=[jax.ShapeDtypeStruct(x.shape, F32),
                   jax.ShapeDtypeStruct((n_b, CONV_CTX, D_MODEL), F32)],
        scratch_shapes=[pltpu.VMEM((tile_rows + CONV_HALO, D_MODEL), F32),
                        pltpu.VMEM((tile_rows, D_MODEL), F32)],
        compiler_params=_params("arbitrary", "arbitrary"),
        name=f"conv_p_l{layer}",
    )(x, mod, w_pw1, b_pw1, w_dw, b_dw, cln_g, cln_b, w_pw2, b_pw2, mix_g, mix_b)


def _conv_s_kernel(x_ref, mod_ref, st_ref, wpw1_ref, bpw1_ref, wdw_ref, bdw_ref, cg_ref, cb_ref,
                   wpw2_ref, bpw2_ref, g_ref, b_ref, o_ref, stn_ref):
    g, r, _ = x_ref.shape
    xs = [x_ref[:, t, :] for t in range(r)]
    mods = [mod_ref[:, 0, k * D_MODEL:(k + 1) * D_MODEL] for k in range(3)]
    h = jnp.concatenate([x * (1.0 + mods[1]) + mods[0] for x in xs], axis=0).astype(BF16)
    a = _glu_rows(h, wpw1_ref, bpw1_ref)
    full = [st_ref[i] for i in range(CONV_CTX)] + [a[t * g:(t + 1) * g] for t in range(r)]
    ys = []
    for t in range(r):
        y = jnp.broadcast_to(bdw_ref[...], (g, D_MODEL))
        for k in range(CONV_WIDTH):
            y = y + wdw_ref[k:k + 1, :] * full[t + k]
        ys.append(y)
    y = _layer_norm(jnp.concatenate(ys, axis=0), cg_ref[...], cb_ref[...])
    y = (y * jax.nn.sigmoid(y)).astype(BF16)
    out = _dot(y, wpw2_ref[...]) + bpw2_ref[...]
    for t in range(r):
        res = ALPHA * xs[t] + mods[2] * out[t * g:(t + 1) * g]
        o_ref[:, t, :] = _layer_norm(res, g_ref[...], b_ref[...])
    for i in range(CONV_CTX):
        stn_ref[i] = full[i + r]


def _conv_s_call(x, mod, layer, j, state_t, w_pw1, b_pw1, w_dw, b_dw, cln_g, cln_b, w_pw2, b_pw2, mix_g, mix_b,
                 tile_rows):
    s = _Stream(x, tile_rows)
    st_spec = pl.BlockSpec((None, CONV_CTX, s.g, D_MODEL), lambda i: (j, 0, i, 0))
    stn_spec = pl.BlockSpec((CONV_CTX, s.g, D_MODEL), lambda i: (0, i, 0))
    return pl.pallas_call(
        _conv_s_kernel,
        grid=(s.n_tiles,),
        in_specs=[s.x_spec(), s.mod_spec(layer), st_spec, _layer_spec(w_pw1, j), _layer_spec(b_pw1, j),
                  _layer_spec(w_dw, j), _layer_spec(b_dw, j), _layer_spec(cln_g, j), _layer_spec(cln_b, j),
                  _layer_spec(w_pw2, j), _layer_spec(b_pw2, j), _layer_spec(mix_g, layer),
                  _layer_spec(mix_b, layer)],
        out_specs=[s.x_spec(), stn_spec],
        out_shape=[s.x_shape(), jax.ShapeDtypeStruct((CONV_CTX, s.n_groups, D_MODEL), F32)],
        compiler_params=_params("arbitrary"),
        name=f"conv_s_l{layer}",
    )(x, mod, state_t, w_pw1, b_pw1, w_dw, b_dw, cln_g, cln_b, w_pw2, b_pw2, mix_g, mix_b)


def _rope_tile(x, cos, sin_signed):
    lane = lax.broadcasted_iota(jnp.int32, x.shape, 1)
    upper = pltpu.roll(x, LANES - MLA_ROPE // 2, 1)
    lower = pltpu.roll(x, MLA_ROPE // 2, 1)
    swapped = jnp.where((lane % MLA_ROPE) < MLA_ROPE // 2, upper, lower)
    return x * cos + swapped * sin_signed


MLA_QK = MLA_KV_RANK + MLA_ROPE


def _mla_proj_kernel(merged, x_ref, mod_ref, wdown_ref, gq_ref, gkv_ref, wuq_ref, wuk_ref, cos_ref, sin_ref,
                     *out_refs):
    if merged:
        q_ref, ckv_ref, kr_ref, kcat_ref = out_refs
        qa_dst = lambda hd: q_ref.at[hd, :, :MLA_KV_RANK]
        qr_dst = lambda hd: q_ref.at[hd, :, MLA_KV_RANK:]
        q_dtype = q_ref.dtype
    else:
        qa_ref, qr_ref, ckv_ref, kr_ref = out_refs
        qa_dst = lambda hd: qa_ref.at[hd]
        qr_dst = lambda hd: qr_ref.at[hd]
        q_dtype = qa_ref.dtype
    x = x_ref[...]
    h = _modulated_rows(x, mod_ref, 0, 1)
    d = _dot(h, wdown_ref[...])
    q_lat = _rms_norm(d[:, :MLA_Q_RANK], gq_ref[...])
    ckv = _rms_norm(d[:, MLA_Q_RANK:MLA_Q_RANK + MLA_KV_RANK], gkv_ref[...])
    cos = cos_ref[...]
    sin = sin_ref[...]
    kr = _rope_tile(d[:, MLA_Q_RANK + MLA_KV_RANK:], cos, sin)[:, :MLA_ROPE]
    ckv_ref[...] = ckv
    kr_ref[...] = kr
    if merged:
        kcat_ref[:, :MLA_KV_RANK] = ckv.astype(BF16)
        kcat_ref[:, MLA_KV_RANK:] = kr.astype(BF16)
    q = _dot(q_lat.astype(BF16), wuq_ref[...])
    n_nope = MLA_HEADS * MLA_NOPE
    for hd in range(MLA_HEADS):
        qn = q[:, hd * MLA_NOPE:(hd + 1) * MLA_NOPE].astype(BF16)
        qa_dst(hd)[...] = (_dot(qn, wuk_ref[hd]) * MLA_SCALE).astype(q_dtype)
    for t in range(MLA_HEADS * MLA_ROPE // LANES):
        rot = _rope_tile(q[:, n_nope + t * LANES:n_nope + (t + 1) * LANES], cos, sin) * MLA_SCALE
        qr_dst(2 * t)[...] = rot[:, :MLA_ROPE].astype(q_dtype)
        qr_dst(2 * t + 1)[...] = rot[:, MLA_ROPE:].astype(q_dtype)


def _mla_proj_call(x, mod, layer, j, w_down, g_q, g_kv, w_uq, w_uk, cos, sin, tile_rows, merged, pos_tiles):
    s = _Stream(x, tile_rows)
    n = s.n_tokens
    if pos_tiles == 1:
        pos_spec = pl.BlockSpec((s.rows, LANES), lambda i: (0, 0))
    else:
        pos_spec = pl.BlockSpec((s.rows, LANES), lambda i: (i % pos_tiles, 0))
    if merged:
        out_specs = [pl.BlockSpec((MLA_HEADS, s.rows, MLA_QK), lambda i: (0, i, 0))]
        out_shape = [jax.ShapeDtypeStruct((MLA_HEADS, n, MLA_QK), BF16)]
    else:
        out_specs = [pl.BlockSpec((MLA_HEADS, s.rows, MLA_KV_RANK), lambda i: (0, i, 0)),
                     pl.BlockSpec((MLA_HEADS, s.rows, MLA_ROPE), lambda i: (0, i, 0))]
        out_shape = [jax.ShapeDtypeStruct((MLA_HEADS, n, MLA_KV_RANK), F32),
                     jax.ShapeDtypeStruct((MLA_HEADS, n, MLA_ROPE), F32)]
    out_specs += [s.rows_spec(MLA_KV_RANK), s.rows_spec(MLA_ROPE)]
    out_shape += [jax.ShapeDtypeStruct((n, MLA_KV_RANK), F32), jax.ShapeDtypeStruct((n, MLA_ROPE), F32)]
    if merged:
        out_specs.append(s.rows_spec(MLA_QK))
        out_shape.append(jax.ShapeDtypeStruct((n, MLA_QK), BF16))
    return pl.pallas_call(
        functools.partial(_mla_proj_kernel, merged),
        grid=(s.n_tiles,),
        in_specs=[s.x_spec(), s.mod_spec(layer), _layer_spec(w_down, j), _layer_spec(g_q, j),
                  _layer_spec(g_kv, j), _layer_spec(w_uq, j), _layer_spec(w_uk, j), pos_spec, pos_spec],
        out_specs=out_specs,
        out_shape=out_shape,
        compiler_params=_params("arbitrary"),
        name=f"mla_proj_l{layer}_g{s.g}",
    )(x, mod, w_down, g_q, g_kv, w_uq, w_uk, cos, sin)


ATTN_Q_TILE = 512
ATTN_K_TILE = 512
ATTN_ROW_PARTS = 8


def _attn_p_kernel(q_ref, k_ref, x_ref, mod_ref, wuv_ref, wo_ref, g_ref, b_ref, o_ref,
                   m_scr, l_scr, acc_scr, s_scr):
    i = pl.program_id(1)
    heads, tq, _ = q_ref.shape
    rows = heads * tq
    tk = ATTN_K_TILE
    m_scr[...] = jnp.full(m_scr.shape, NEG_BIG, F32)
    l_scr[...] = jnp.zeros(l_scr.shape, F32)
    acc_scr[...] = jnp.zeros(acc_scr.shape, F32)
    part_rows = rows // ATTN_ROW_PARTS
    parts = [slice(p * part_rows, (p + 1) * part_rows) for p in range(ATTN_ROW_PARTS)]

    def q_part(rs):
        return q_ref[rs.start // tq:rs.stop // tq].reshape(part_rows, MLA_QK)

    def keys(jb):
        return k_ref[pl.ds(pl.multiple_of(jb * tk, tk), tk), :]

    def absorb(rs, s, vb, masked):
        if masked:
            q_pos = lax.broadcasted_iota(jnp.int32, s.shape, 0) % tq
            k_pos = lax.broadcasted_iota(jnp.int32, s.shape, 1)
            s = jnp.where(k_pos <= q_pos, s, NEG_BIG)
        m_prev = m_scr[rs, :]
        m_next = jnp.maximum(m_prev, jnp.max(s, axis=1, keepdims=True))
        p = jnp.exp(s - jnp.concatenate([m_next] * (tk // LANES), axis=1))
        alpha = jnp.exp(m_prev - m_next)
        l_scr[rs, :] = alpha * l_scr[rs, :] + jnp.sum(p, axis=1, keepdims=True)
        m_scr[rs, :] = m_next
        acc_scr[rs, :] = (acc_scr[rs, :] * jnp.concatenate([alpha] * (MLA_KV_RANK // LANES), axis=1)
                          + _dot(p.astype(BF16), vb))

    kb0 = keys(0)
    for rs in parts:
        s_scr[rs, :] = _dot_nt(q_part(rs), kb0)

    def body(jb, carry):
        vb = keys(jb)[:, :MLA_KV_RANK]
        kb_next = keys(jb + 1)
        for rs in parts:
            s = s_scr[rs, :]
            s_scr[rs, :] = _dot_nt(q_part(rs), kb_next)
            absorb(rs, s, vb, False)
        return carry

    lax.fori_loop(0, i, body, 0)
    vb = keys(i)[:, :MLA_KV_RANK]
    for rs in parts:
        absorb(rs, s_scr[rs, :], vb, True)
    o_heads = []
    for hd in range(heads):
        rs = slice(hd * tq, (hd + 1) * tq)
        inv = 1.0 / l_scr[rs, :]
        ctx = (acc_scr[rs, :] * jnp.concatenate([inv] * (MLA_KV_RANK // LANES), axis=1)).astype(BF16)
        o_heads.append(_dot(ctx, wuv_ref[hd]).astype(BF16))
    y = _dot(jnp.concatenate(o_heads, axis=1), wo_ref[...])
    o_ref[...] = _residual_norm(x_ref[...], y, _mod(mod_ref, 2), g_ref, b_ref)


def _attn_p_call(q, kcat, x, mod, layer, j, w_uv, w_o, mix_g, mix_b):
    assert ATTN_Q_TILE == ATTN_K_TILE
    n_batch, seq, _ = x.shape
    nq = seq // ATTN_Q_TILE
    rows = MLA_HEADS * ATTN_Q_TILE
    xspec = pl.BlockSpec((1, ATTN_Q_TILE, D_MODEL), lambda b, i: (b, i, 0))
    return pl.pallas_call(
        _attn_p_kernel,
        grid=(n_batch, nq),
        in_specs=[pl.BlockSpec((MLA_HEADS, ATTN_Q_TILE, MLA_QK), lambda b, i: (0, b * nq + i, 0)),
                  pl.BlockSpec((seq, MLA_QK), lambda b, i: (b, 0)),
                  xspec,
                  pl.BlockSpec((None, 1, 1, 6 * D_MODEL), lambda b, i: (layer, b, 0, 0)),
                  _layer_spec(w_uv, j), _layer_spec(w_o, j), _layer_spec(mix_g, layer),
                  _layer_spec(mix_b, layer)],
        out_specs=xspec,
        out_shape=jax.ShapeDtypeStruct(x.shape, F32),
        scratch_shapes=[pltpu.VMEM((rows, LANES), F32), pltpu.VMEM((rows, LANES), F32),
                        pltpu.VMEM((rows, MLA_KV_RANK), F32), pltpu.VMEM((rows, ATTN_K_TILE), F32)],
        compiler_params=_params("arbitrary", "arbitrary"),
        name="mla_attn_p",
    )(q, kcat, x, mod, w_uv, w_o, mix_g, mix_b)


ATTN_S_SPLITS = 4


def _attn_s_kernel(j, pt_ref, qa_ref, qr_ref, kn_ref, rn_ref, cache_k, cache_rt, o_ref, kbuf, rbuf, sems):
    b = pl.program_id(0)
    nb = pl.num_programs(0)
    n_pages = pt_ref.shape[1]
    page = cache_k.shape[2]
    slot = b % 2

    def copies(pg, sl, p):
        keys = pl.ds(p * page, page)
        return (pltpu.make_async_copy(cache_k.at[j, pg], kbuf.at[sl, keys, :], sems.at[0, sl]),
                pltpu.make_async_copy(cache_rt.at[j, pg], rbuf.at[sl, :, keys], sems.at[1, sl]))

    def start_all(bi, sl):
        for p in range(n_pages):
            for c in copies(pt_ref[bi, p], sl, p):
                c.start(priority=p % 2)

    def wait_all(sl):
        for p in range(n_pages):
            for c in copies(0, sl, p):
                c.wait()

    @pl.when(b == 0)
    def _():
        start_all(0, 0)

    @pl.when(b + 1 < nb)
    def _():
        start_all(b + 1, 1 - slot)

    wait_all(slot)

    heads, t, _ = qa_ref.shape
    rows = heads * t
    qa = qa_ref[...].reshape(rows, MLA_KV_RANK).astype(BF16)
    qr = qr_ref[...].reshape(rows, MLA_ROPE).astype(BF16)
    kn = kn_ref[...].astype(BF16)
    rn = rn_ref[...].astype(BF16)

    s_new = _dot_nt(qa, kn) + _dot_nt(qr, rn)
    q_pos = lax.broadcasted_iota(jnp.int32, s_new.shape, 0) % t
    k_pos = lax.broadcasted_iota(jnp.int32, s_new.shape, 1)
    s_new = jnp.where(k_pos <= q_pos, s_new, NEG_BIG)
    m_new = jnp.max(s_new, axis=1, keepdims=True)
    p_new = jnp.exp(s_new - m_new)
    parts = [(m_new, jnp.sum(p_new, axis=1, keepdims=True), _dot(p_new.astype(BF16), kn))]

    span = n_pages * page // ATTN_S_SPLITS
    chunks = [slice(c * span, (c + 1) * span) for c in range(ATTN_S_SPLITS)]
    kps = [kbuf[slot, keys, :].astype(BF16) for keys in chunks]
    ss = [_dot_nt(qa, kp) + _dot(qr, rbuf[slot, :, keys].astype(BF16)) for kp, keys in zip(kps, chunks)]
    ms = [jnp.max(s, axis=1, keepdims=True) for s in ss]
    ps = [jnp.exp(s - m_c) for s, m_c in zip(ss, ms)]
    for m_c, p, kp in zip(ms, ps, kps):
        parts.append((m_c, jnp.sum(p, axis=1, keepdims=True), _dot(p.astype(BF16), kp)))

    m = functools.reduce(jnp.maximum, [pt[0] for pt in parts])
    scales = [jnp.exp(pt[0] - m) for pt in parts]
    denom = sum(sc * pt[1] for sc, pt in zip(scales, parts))
    ctx = sum(sc * pt[2] for sc, pt in zip(scales, parts)) / denom
    for hd in range(heads):
        o_ref[:, hd * MLA_KV_RANK:(hd + 1) * MLA_KV_RANK] = ctx[hd * t:(hd + 1) * t]


def _attn_s_call(page_table, qa, qr, ckv_new, kr_new, cache_k, cache_r, j, n_batch, t):
    past = page_table.shape[1] * cache_k.shape[2]
    grid_spec = pltpu.PrefetchScalarGridSpec(
        num_scalar_prefetch=1,
        grid=(n_batch,),
        in_specs=[pl.BlockSpec((MLA_HEADS, t, MLA_KV_RANK), lambda b, pt: (0, b, 0)),
                  pl.BlockSpec((MLA_HEADS, t, MLA_ROPE), lambda b, pt: (0, b, 0)),
                  pl.BlockSpec((t, MLA_KV_RANK), lambda b, pt: (b, 0)),
                  pl.BlockSpec((t, MLA_ROPE), lambda b, pt: (b, 0)),
                  pl.BlockSpec(memory_space=pl.ANY),
                  pl.BlockSpec(memory_space=pl.ANY)],
        out_specs=pl.BlockSpec((t, MLA_HEADS * MLA_KV_RANK), lambda b, pt: (b, 0)),
        scratch_shapes=[pltpu.VMEM((2, past, MLA_KV_RANK), F32),
                        pltpu.VMEM((2, MLA_ROPE, past), F32),
                        pltpu.SemaphoreType.DMA((2, 2))],
    )
    return pl.pallas_call(
        functools.partial(_attn_s_kernel, j),
        grid_spec=grid_spec,
        out_shape=jax.ShapeDtypeStruct((n_batch * t, MLA_HEADS * MLA_KV_RANK), F32),
        compiler_params=_params("arbitrary"),
        name="mla_attn_s",
    )(page_table, qa, qr, ckv_new, kr_new, cache_k, cache_r)


def _mla_out_kernel(x_ref, mod_ref, ctx_ref, wuv_ref, wo_ref, g_ref, b_ref, o_ref):
    x = x_ref[...]
    parts = []
    for hd in range(MLA_HEADS):
        c = ctx_ref[:, hd * MLA_KV_RANK:(hd + 1) * MLA_KV_RANK].astype(BF16)
        parts.append(_dot(c, wuv_ref[hd]).astype(BF16))
    o = jnp.concatenate(parts, axis=1)
    y = _dot(o, wo_ref[...])
    o_ref[...] = _residual_norm(x, y, _mod(mod_ref, 2), g_ref, b_ref)


def _mla_out_call(x, mod, layer, j, ctx, w_uv, w_o, mix_g, mix_b, tile_rows):
    s = _Stream(x, tile_rows)
    return pl.pallas_call(
        _mla_out_kernel,
        grid=(s.n_tiles,),
        in_specs=[s.x_spec(), s.mod_spec(layer), s.rows_spec(MLA_HEADS * MLA_KV_RANK),
                  _layer_spec(w_uv, j), _layer_spec(w_o, j), _layer_spec(mix_g, layer),
                  _layer_spec(mix_b, layer)],
        out_specs=s.x_spec(),
        out_shape=s.x_shape(),
        compiler_params=_params("arbitrary"),
        name=f"mla_out_l{layer}_g{s.g}",
    )(x, mod, ctx, w_uv, w_o, mix_g, mix_b)


def _row3(a):
    return a.reshape(a.shape[0], 1, a.shape[1])


def _rope_tables(pos):
    half = MLA_ROPE // 2
    inv = ROPE_THETA ** (-jnp.arange(half, dtype=F32) / half)
    ang = pos.astype(F32)[:, None] * inv[None, :]
    cos, sin = jnp.cos(ang), jnp.sin(ang)
    cos64 = jnp.concatenate([cos, cos], -1)
    sin64 = jnp.concatenate([-sin, sin], -1)
    return jnp.tile(cos64, (1, LANES // MLA_ROPE)), jnp.tile(sin64, (1, LANES // MLA_ROPE))


def kernel(x_prompt, x_sample, cache_ckv, cache_krope, state_conv, page_table, c_prompt, c_sample,
           ada_w, ada_b, ln_mix_g, ln_mix_b, ln_ffn_g, ln_ffn_b, ffn_w_gate, ffn_w_up, ffn_w_down,
           gmlp_w_in, gmlp_ln_g, gmlp_ln_b, gmlp_w_s, gmlp_b_s, gmlp_w_out,
           conv_w_pw1, conv_b_pw1, conv_w_dw, conv_b_dw, conv_ln_g, conv_ln_b, conv_w_pw2, conv_b_pw2,
           mla_w_down, mla_g_q, mla_g_kv, mla_w_uq, mla_w_uk, mla_w_uv, mla_w_o):
    n_b, seq, _ = x_prompt.shape
    n_db, dec_seq, _ = x_sample.shape
    past_len = page_table.shape[1] * cache_ckv.shape[2]

    c_p = jnp.pad(c_prompt, ((0, SUBLANES - n_b), (0, 0)))
    mod_p, mod_s = _ada_call(c_p, c_sample, ada_w, ada_b)
    mod_p = mod_p.reshape(DEPTH, SUBLANES, 1, 6 * D_MODEL)
    mod_s = mod_s.reshape(DEPTH, n_db, 1, 6 * D_MODEL)

    ln_mix_g, ln_mix_b, ln_ffn_g, ln_ffn_b = map(_row3, (ln_mix_g, ln_mix_b, ln_ffn_g, ln_ffn_b))
    wg, wu, wd = (w.astype(BF16) for w in (ffn_w_gate, ffn_w_up, ffn_w_down))

    g_w_in, g_w_out = gmlp_w_in.astype(BF16), gmlp_w_out.astype(BF16)
    g_ln_g, g_ln_b = _row3(gmlp_ln_g), _row3(gmlp_ln_b)
    reps = GMLP_CHUNK // dec_seq
    eye = jnp.eye(reps, dtype=F32)

    c_w_pw1, c_w_pw2 = conv_w_pw1.astype(BF16), conv_w_pw2.astype(BF16)
    c_b_pw1, c_b_dw, c_ln_g, c_ln_b, c_b_pw2 = map(_row3, (conv_b_pw1, conv_b_dw, conv_ln_g, conv_ln_b,
                                                            conv_b_pw2))
    c_w_dw8 = jnp.broadcast_to(conv_w_dw[:, :, None, :], conv_w_dw.shape[:2] + (SUBLANES, D_MODEL))
    state_t = jnp.swapaxes(state_conv, 1, 2)

    n_mla = mla_w_down.shape[0]
    m_w_down = jnp.pad(mla_w_down, ((0, 0), (0, 0), (0, MLA_DOWN_PAD - mla_w_down.shape[-1]))).astype(BF16)
    w_uq = mla_w_uq.reshape(n_mla, MLA_Q_RANK, MLA_HEADS, MLA_NOPE + MLA_ROPE)
    m_w_uq = jnp.concatenate([w_uq[..., :MLA_NOPE].reshape(n_mla, MLA_Q_RANK, -1),
                              w_uq[..., MLA_NOPE:].reshape(n_mla, MLA_Q_RANK, -1)], -1).astype(BF16)
    m_w_uk = mla_w_uk.transpose(0, 2, 3, 1).astype(BF16)
    m_w_uv = mla_w_uv.transpose(0, 2, 1, 3).astype(BF16)
    m_w_o = mla_w_o.astype(BF16)
    m_g_q, m_g_kv = _row3(mla_g_q), _row3(mla_g_kv)
    cos_p, sin_p = _rope_tables(jnp.arange(seq))
    cos_s, sin_s = _rope_tables(past_len + jnp.arange(dec_seq))

    ffn_tile = 512
    gm_tile = 512
    conv_tile = 512
    mla_tile = 512
    s_reps = mla_tile // dec_seq
    cos_s, sin_s = jnp.tile(cos_s, (s_reps, 1)), jnp.tile(sin_s, (s_reps, 1))

    xp, xs = x_prompt, x_sample
    ckv_p_rows, kr_p_rows, ckv_s_rows, kr_s_rows = [], [], [], []
    conv_p_states, conv_s_states, gmlp_v_rows = [], [], []
    for i in range(DEPTH):
        kind, j = i % N_MIXERS, i // N_MIXERS
        if kind == 0:
            wmix_p = gmlp_w_s[j].astype(BF16)
            bmix_p = gmlp_b_s[j].T
            corner = gmlp_w_s[j, :, :dec_seq, :dec_seq]
            wmix_s = jnp.einsum('ab,gts->gatbs', eye, corner).reshape(
                GMLP_GROUPS, GMLP_CHUNK, GMLP_CHUNK).astype(BF16)
            bmix_s = jnp.tile(gmlp_b_s[j, :, :dec_seq].T, (reps, 1))
            xp, xs, v_s = _gmlp_call(xp, mod_p, xs, mod_s, i, j, g_w_in, g_ln_g, g_ln_b, wmix_p, bmix_p,
                                     wmix_s, bmix_s, g_w_out, ln_mix_g, ln_mix_b, gm_tile)
            gmlp_v_rows.append(v_s)
        elif kind == 1:
            xp, st_p = _conv_p_call(xp, mod_p, i, j, c_w_pw1, c_b_pw1, c_w_dw8, c_b_dw, c_ln_g, c_ln_b,
                                    c_w_pw2, c_b_pw2, ln_mix_g, ln_mix_b, conv_tile)
            xs, st_s = _conv_s_call(xs, mod_s, i, j, state_t, c_w_pw1, c_b_pw1, conv_w_dw, c_b_dw,
                                    c_ln_g, c_ln_b, c_w_pw2, c_b_pw2, ln_mix_g, ln_mix_b, conv_tile)
            conv_p_states.append(st_p)
            conv_s_states.append(jnp.swapaxes(st_s, 0, 1))
        else:
            q_p, ckv, kr, kcat = _mla_proj_call(
                xp, mod_p, i, j, m_w_down, m_g_q, m_g_kv, m_w_uq, m_w_uk, cos_p, sin_p, mla_tile, True,
                seq // mla_tile)
            xp = _attn_p_call(q_p, kcat, xp, mod_p, i, j, m_w_uv, m_w_o, ln_mix_g, ln_mix_b)
            ckv_p_rows.append(ckv.reshape(n_b, seq, MLA_KV_RANK))
            kr_p_rows.append(kr.reshape(n_b, seq, MLA_ROPE))
            qa, qr, ckv, kr = _mla_proj_call(
                xs, mod_s, i, j, m_w_down, m_g_q, m_g_kv, m_w_uq, m_w_uk, cos_s, sin_s, mla_tile, False, 1)
            ctx_s = _attn_s_call(page_table, qa, qr, ckv, kr, cache_ckv, jnp.swapaxes(cache_krope, 2, 3), j,
                                 n_db, dec_seq)
            xs = _mla_out_call(xs, mod_s, i, j, ctx_s, m_w_uv, m_w_o, ln_mix_g, ln_mix_b, mla_tile)
            ckv_s_rows.append(ckv.reshape(n_db, dec_seq, MLA_KV_RANK))
            kr_s_rows.append(kr.reshape(n_db, dec_seq, MLA_ROPE))
        xp, xs = _ffn_call(xp, mod_p, xs, mod_s, i, wg, wu, wd, ln_ffn_g, ln_ffn_b, ffn_tile)
    return (xp, xs, jnp.stack(ckv_p_rows), jnp.stack(kr_p_rows), jnp.stack(ckv_s_rows), jnp.stack(kr_s_rows),
            jnp.stack(conv_p_states), jnp.stack(conv_s_states), jnp.stack(gmlp_v_rows))
```

```python
import functools
import math

import jax
import jax.numpy as jnp
from jax import lax
from jax.experimental import pallas as pl
from jax.experimental.pallas import tpu as pltpu

F32 = jnp.float32
BF16 = jnp.bfloat16

D_MODEL = 1024
DEPTH = 4
N_MIXERS = 3
ALPHA = (2.0 * DEPTH) ** 0.25
LN_EPS = 1e-5
RMS_EPS = 1e-6

GMLP_DIM = 2 * D_MODEL
GMLP_GROUPS = 8
GMLP_CHUNK = 128
GMLP_GROUP_WIDTH = GMLP_DIM // GMLP_GROUPS

CONV_WIDTH = 31
CONV_CTX = CONV_WIDTH - 1

MLA_HEADS = 8
MLA_Q_RANK = 384
MLA_KV_RANK = 256
MLA_NOPE = 128
MLA_ROPE = 64
MLA_V = 128
ROPE_THETA = 10000.0
MLA_SCALE = (MLA_NOPE + MLA_ROPE) ** -0.5
MLA_DOWN_PAD = MLA_Q_RANK + MLA_KV_RANK + 128

FFN_DIM = ((8 * D_MODEL + 3 * 256 - 1) // (3 * 256)) * 256

VMEM_LIMIT_BYTES = 56 * 1024 * 1024
LANES = 128
SUBLANES = 8

NEG_BIG = -1e30


def _params(*semantics):
    return pltpu.CompilerParams(dimension_semantics=semantics, vmem_limit_bytes=VMEM_LIMIT_BYTES)


def _dot(a, b):
    return jnp.dot(a, b, preferred_element_type=F32)


def _dot_nt(a, b):
    return lax.dot_general(a, b, (((1,), (1,)), ((), ())), preferred_element_type=F32)


def _layer_norm(x, g, b):
    mu = jnp.mean(x, -1, keepdims=True)
    xc = x - mu
    var = jnp.mean(xc * xc, -1, keepdims=True)
    return xc * lax.rsqrt(var + LN_EPS) * g + b


def _rms_norm(x, g):
    return x * lax.rsqrt(jnp.mean(x * x, -1, keepdims=True) + RMS_EPS) * g


def _mod(mod_ref, k, groups=slice(None)):
    return mod_ref[groups, :, k * D_MODEL:(k + 1) * D_MODEL]


def _modulated_rows(x, mod_ref, k_shift, k_scale, groups=slice(None)):
    g, r, d = x.shape
    h = x * (1.0 + _mod(mod_ref, k_scale, groups)) + _mod(mod_ref, k_shift, groups)
    return h.reshape(g * r, d).astype(BF16)


def _row_parts(x_ref, n):
    g, r, _ = x_ref.shape
    if g == 1:
        return [(slice(0, 1), slice(p * (r // n), (p + 1) * (r // n))) for p in range(n)]
    return [(slice(p * (g // n), (p + 1) * (g // n)), slice(0, r)) for p in range(n)]


def _residual_norm(x, y_rows, gate, g_ref, b_ref):
    return _layer_norm(ALPHA * x + gate * y_rows.reshape(x.shape), g_ref[...], b_ref[...])


def _ada_kernel(cp_ref, cs_ref, w_ref, b_ref, mp_ref, ms_ref):
    w = w_ref[...].astype(BF16)
    bias = b_ref[...]
    for c_ref, o_ref in ((cp_ref, mp_ref), (cs_ref, ms_ref)):
        c = c_ref[...]
        a = (c * jax.nn.sigmoid(c)).astype(BF16)
        o_ref[...] = (_dot(a, w) + bias).reshape(o_ref.shape)


def _ada_call(c_p, c_s, ada_w, ada_b):
    tn = 3072
    n_p, n_s = c_p.shape[0], c_s.shape[0]
    d6 = ada_w.shape[-1]
    return pl.pallas_call(
        _ada_kernel,
        grid=(DEPTH, d6 // tn),
        in_specs=[
            pl.BlockSpec((n_p, D_MODEL), lambda l, n: (0, 0)),
            pl.BlockSpec((n_s, D_MODEL), lambda l, n: (0, 0)),
            pl.BlockSpec((None, D_MODEL, tn), lambda l, n: (l, 0, n)),
            pl.BlockSpec((None, 1, tn), lambda l, n: (l, 0, n)),
        ],
        out_specs=[
            pl.BlockSpec((None, n_p, 1, tn), lambda l, n: (l, 0, 0, n)),
            pl.BlockSpec((None, n_s, 1, tn), lambda l, n: (l, 0, 0, n)),
        ],
        out_shape=[
            jax.ShapeDtypeStruct((DEPTH, n_p, 1, d6), F32),
            jax.ShapeDtypeStruct((DEPTH, n_s, 1, d6), F32),
        ],
        compiler_params=_params("arbitrary", "arbitrary"),
        name="ada_table",
    )(c_p, c_s, ada_w, ada_b.reshape(DEPTH, 1, d6))


class _Stream:
    def __init__(self, x, tile_rows, first=None):
        self.first = first
        self.n_groups, self.group_rows, _ = x.shape
        if self.group_rows >= tile_rows:
            assert self.group_rows % tile_rows == 0
            self.g, self.r = 1, tile_rows
            self.tiles_per_group = self.group_rows // tile_rows
        else:
            assert tile_rows % self.group_rows == 0
            self.g, self.r = tile_rows // self.group_rows, self.group_rows
            assert self.n_groups % self.g == 0
            self.tiles_per_group = 1
        self.rows = self.g * self.r
        self.n_tiles = (self.n_groups // self.g) * self.tiles_per_group
        self.n_tokens = self.n_groups * self.group_rows

    def _tile(self, i):
        return i if self.first is None else jnp.clip(i - self.first, 0, self.n_tiles - 1)

    def _gi(self, i):
        t = self._tile(i)
        if self.tiles_per_group == 1:
            return t, 0
        return t // self.tiles_per_group, t % self.tiles_per_group

    def x_spec(self, width=D_MODEL):
        return pl.BlockSpec((self.g, self.r, width), lambda i: (*self._gi(i), 0))

    def mod_spec(self, layer):
        return pl.BlockSpec((None, self.g, 1, 6 * D_MODEL), lambda i: (layer, self._gi(i)[0], 0, 0))

    def rows_spec(self, width):
        return pl.BlockSpec((self.rows, width), lambda i: (self._tile(i), 0))

    def x_shape(self, width=D_MODEL, dtype=F32):
        return jax.ShapeDtypeStruct((self.n_groups, self.group_rows, width), dtype)


def _const_spec(shape):
    zeros = (0,) * len(shape)
    return pl.BlockSpec(shape, lambda *_: zeros, pipeline_mode=pl.Buffered(1))


def _layer_spec(arr, layer):
    zeros = (0,) * (arr.ndim - 1)
    return pl.BlockSpec((None,) + arr.shape[1:], lambda *_: (layer,) + zeros,
                        pipeline_mode=pl.Buffered(1))


MXU_WIDTH = 256
FFN_SPLIT = (FFN_DIM // MXU_WIDTH + 1) // 2 * MXU_WIDTH


FFN_ROW_PARTS = 2
GMLP_ROW_PARTS = 2
CONV_ROW_PARTS = 2


def _ffn_tile(x_ref, mod_ref, wg_ref, wu_ref, wd_ref, g_ref, b_ref, o_ref, a_scr):
    parts = _row_parts(x_ref, FFN_ROW_PARTS)
    xs = [x_ref[gs, rs, :] for gs, rs in parts]
    hs = [_modulated_rows(x, mod_ref, 3, 4, gs) for x, (gs, _) in zip(xs, parts)]
    fs = []
    r0 = 0
    for h in hs:
        rows = h.shape[0]
        for sl in (slice(0, FFN_SPLIT), slice(FFN_SPLIT, FFN_DIM)):
            gate = _dot(h, wg_ref[:, sl])
            up = _dot(h, wu_ref[:, sl])
            a_scr[r0:r0 + rows, sl] = (gate * jax.nn.sigmoid(gate) * up).astype(BF16)
        fs.append(_dot(a_scr[r0:r0 + rows, :], wd_ref[...]))
        r0 += rows
    for x, f, (gs, rs) in zip(xs, fs, parts):
        o_ref[gs, rs, :] = _residual_norm(x, f, _mod(mod_ref, 5, gs), g_ref, b_ref)


def _ffn_kernel(n_p, xp_ref, modp_ref, xs_ref, mods_ref, wg_ref, wu_ref, wd_ref, g_ref, b_ref,
                op_ref, os_ref, a_scr):
    i = pl.program_id(0)

    @pl.when(i < n_p)
    def _():
        _ffn_tile(xp_ref, modp_ref, wg_ref, wu_ref, wd_ref, g_ref, b_ref, op_ref, a_scr)

    @pl.when(i >= n_p)
    def _():
        _ffn_tile(xs_ref, mods_ref, wg_ref, wu_ref, wd_ref, g_ref, b_ref, os_ref, a_scr)


def _ffn_call(xp, mod_p, xs, mod_s, layer, wg, wu, wd, ln_g, ln_b, tile_rows):
    sp = _Stream(xp, tile_rows, first=0)
    ss = _Stream(xs, tile_rows, first=sp.n_tiles)
    assert sp.rows == ss.rows
    return pl.pallas_call(
        functools.partial(_ffn_kernel, sp.n_tiles),
        grid=(sp.n_tiles + ss.n_tiles,),
        in_specs=[sp.x_spec(), sp.mod_spec(layer), ss.x_spec(), ss.mod_spec(layer),
                  _layer_spec(wg, layer), _layer_spec(wu, layer), _layer_spec(wd, layer),
                  _layer_spec(ln_g, layer), _layer_spec(ln_b, layer)],
        out_specs=[sp.x_spec(), ss.x_spec()],
        out_shape=[sp.x_shape(), ss.x_shape()],
        scratch_shapes=[pltpu.VMEM((sp.rows, FFN_DIM), BF16)],
        compiler_params=_params("arbitrary"),
        name=f"ffn_l{layer}",
    )(xp, mod_p, xs, mod_s, wg, wu, wd, ln_g, ln_b)


def _gelu(z):
    return 0.5 * z * (1.0 + lax.erf(z * (1.0 / math.sqrt(2.0))))


def _gmlp_tile(x_ref, mod_ref, win_ref, lng_ref, lnb_ref, wmix_ref, bmix_ref, wout_ref, g_ref, b_ref,
               o_ref, v_ref, um_scr):
    row_id = lax.broadcasted_iota(jnp.int32, (GMLP_CHUNK, GMLP_CHUNK), 0)
    col_id = lax.broadcasted_iota(jnp.int32, (GMLP_CHUNK, GMLP_CHUNK), 1)
    causal = row_id >= col_id
    parts = _row_parts(x_ref, GMLP_ROW_PARTS)
    xs = [x_ref[gs, rs, :] for gs, rs in parts]
    zs = [_dot(_modulated_rows(x, mod_ref, 0, 1, gs), win_ref[...]) for x, (gs, _) in zip(xs, parts)]
    ys = []
    r0 = 0
    for x, z, (gs, rs) in zip(xs, zs, parts):
        rows = z.shape[0]
        z = _gelu(z)
        u = z[:, :GMLP_DIM]
        v = _layer_norm(z[:, GMLP_DIM:], lng_ref[...], lnb_ref[...])
        if v_ref is not None:
            v_ref[gs, rs, :] = v.reshape(x.shape[:2] + (GMLP_DIM,))
        vb = v.astype(BF16)
        for gi in range(GMLP_GROUPS):
            w = jnp.where(causal, wmix_ref[gi], jnp.zeros((), BF16))
            bias = bmix_ref[:, gi:gi + 1]
            cols = slice(gi * GMLP_GROUP_WIDTH, (gi + 1) * GMLP_GROUP_WIDTH)
            for c in range(rows // GMLP_CHUNK):
                rws = slice(c * GMLP_CHUNK, (c + 1) * GMLP_CHUNK)
                mixed = _dot(w, vb[rws, cols]) + bias
                um_scr[r0 + c * GMLP_CHUNK:r0 + (c + 1) * GMLP_CHUNK, cols] = (u[rws, cols] * mixed).astype(BF16)
        ys.append(_dot(um_scr[r0:r0 + rows, :], wout_ref[...]))
        r0 += rows
    for x, y, (gs, rs) in zip(xs, ys, parts):
        o_ref[gs, rs, :] = _residual_norm(x, y, _mod(mod_ref, 2, gs), g_ref, b_ref)


def _gmlp_kernel(n_p, xp_ref, modp_ref, xs_ref, mods_ref, win_ref, lng_ref, lnb_ref, wmixp_ref, bmixp_ref,
                 wmixs_ref, bmixs_ref, wout_ref, g_ref, b_ref, op_ref, os_ref, v_ref, um_scr):
    i = pl.program_id(0)

    @pl.when(i < n_p)
    def _():
        _gmlp_tile(xp_ref, modp_ref, win_ref, lng_ref, lnb_ref, wmixp_ref, bmixp_ref, wout_ref, g_ref, b_ref,
                   op_ref, None, um_scr)

    @pl.when(i >= n_p)
    def _():
        _gmlp_tile(xs_ref, mods_ref, win_ref, lng_ref, lnb_ref, wmixs_ref, bmixs_ref, wout_ref, g_ref, b_ref,
                   os_ref, v_ref, um_scr)


def _gmlp_call(xp, mod_p, xs, mod_s, layer, j, w_in, ln_g, ln_b, wmix_p, bmix_p, wmix_s, bmix_s, w_out,
               mix_g, mix_b, tile_rows):
    sp = _Stream(xp, tile_rows, first=0)
    ss = _Stream(xs, tile_rows, first=sp.n_tiles)
    assert sp.rows == ss.rows
    return pl.pallas_call(
        functools.partial(_gmlp_kernel, sp.n_tiles),
        grid=(sp.n_tiles + ss.n_tiles,),
        in_specs=[sp.x_spec(), sp.mod_spec(layer), ss.x_spec(), ss.mod_spec(layer),
                  _layer_spec(w_in, j), _layer_spec(ln_g, j), _layer_spec(ln_b, j),
                  _const_spec(wmix_p.shape), _const_spec(bmix_p.shape),
                  _const_spec(wmix_s.shape), _const_spec(bmix_s.shape),
                  _layer_spec(w_out, j), _layer_spec(mix_g, layer), _layer_spec(mix_b, layer)],
        out_specs=[sp.x_spec(), ss.x_spec(), ss.x_spec(GMLP_DIM)],
        out_shape=[sp.x_shape(), ss.x_shape(), ss.x_shape(GMLP_DIM)],
        scratch_shapes=[pltpu.VMEM((sp.rows, GMLP_DIM), BF16)],
        compiler_params=_params("arbitrary"),
        name=f"gmlp_l{layer}",
    )(xp, mod_p, xs, mod_s, w_in, ln_g, ln_b, wmix_p, bmix_p, wmix_s, bmix_s, w_out, mix_g, mix_b)


CONV_HALO = 32
CONV_ROW_CHUNK = 128
CONV_LANE_CHUNK = 256


def _glu_rows(h, wpw1_ref, bpw1_ref):
    a = _dot(h, wpw1_ref[...]) + bpw1_ref[...]
    return a[:, :D_MODEL] * jax.nn.sigmoid(a[:, D_MODEL:])


def _conv_tail(x, y, mod_ref, cg_ref, cb_ref, wpw2_ref, bpw2_ref, g_ref, b_ref):
    y = _layer_norm(y, cg_ref[...], cb_ref[...])
    y = (y * jax.nn.sigmoid(y)).astype(BF16)
    out = _dot(y, wpw2_ref[...]) + bpw2_ref[...]
    return _residual_norm(x, out, _mod(mod_ref, 2), g_ref, b_ref)


def _conv_p_kernel(x_ref, mod_ref, wpw1_ref, bpw1_ref, wdw_ref, bdw_ref, cg_ref, cb_ref, wpw2_ref, bpw2_ref,
                   g_ref, b_ref, o_ref, st_ref, buf_scr, y_scr):
    t = pl.program_id(1)
    rows = x_ref.shape[1]
    part_rows = rows // CONV_ROW_PARTS
    off = CONV_HALO - CONV_CTX

    @pl.when(t == 0)
    def _():
        buf_scr[0:CONV_HALO, :] = jnp.zeros((CONV_HALO, D_MODEL), F32)

    xs = [x_ref[:, p * part_rows:(p + 1) * part_rows, :] for p in range(CONV_ROW_PARTS)]
    for p, x in enumerate(xs):
        h = _modulated_rows(x, mod_ref, 0, 1)
        buf_scr[CONV_HALO + p * part_rows:CONV_HALO + (p + 1) * part_rows, :] = _glu_rows(h, wpw1_ref, bpw1_ref)

    n_q = (CONV_WIDTH + SUBLANES - 1) // SUBLANES
    window = CONV_ROW_CHUNK + CONV_HALO
    tiles = CONV_ROW_CHUNK // SUBLANES
    for p, x in enumerate(xs):
        for lc in range(D_MODEL // CONV_LANE_CHUNK):
            lanes = slice(lc * CONV_LANE_CHUNK, (lc + 1) * CONV_LANE_CHUNK)
            for rc in range(part_rows // CONV_ROW_CHUNK):
                r0 = p * part_rows + rc * CONV_ROW_CHUNK
                win = buf_scr[r0:r0 + window, lanes]
                acc = jnp.broadcast_to(bdw_ref[:, lanes][None], (tiles, SUBLANES, CONV_LANE_CHUNK))
                for r in range(SUBLANES):
                    shift = off + r
                    phase = win if shift % window == 0 else pltpu.roll(win, window - shift, 0)
                    for q in range(n_q):
                        k = SUBLANES * q + r
                        if k < CONV_WIDTH:
                            assert SUBLANES * q + CONV_ROW_CHUNK + shift <= window
                            taps = phase[SUBLANES * q:SUBLANES * q + CONV_ROW_CHUNK, :]
                            acc = acc + wdw_ref[k, :, lanes][None] * taps.reshape(tiles, SUBLANES, CONV_LANE_CHUNK)
                y_scr[r0:r0 + CONV_ROW_CHUNK, lanes] = acc.reshape(CONV_ROW_CHUNK, CONV_LANE_CHUNK)
        o_ref[:, p * part_rows:(p + 1) * part_rows, :] = _conv_tail(
            x, y_scr[p * part_rows:(p + 1) * part_rows, :], mod_ref, cg_ref, cb_ref, wpw2_ref, bpw2_ref, g_ref, b_ref)

    @pl.when(t == pl.num_programs(1) - 1)
    def _():
        st_ref[...] = buf_scr[rows + off:rows + CONV_HALO, :].reshape(st_ref.shape)

    buf_scr[0:CONV_HALO, :] = buf_scr[rows:rows + CONV_HALO, :]


def _conv_p_call(x, mod, layer, j, w_pw1, b_pw1, w_dw, b_dw, cln_g, cln_b, w_pw2, b_pw2, mix_g, mix_b,
                 tile_rows):
    n_b, seq, _ = x.shape
    n_t = seq // tile_rows
    xspec = pl.BlockSpec((1, tile_rows, D_MODEL), lambda b, t: (b, t, 0))
    return pl.pallas_call(
        _conv_p_kernel,
        grid=(n_b, n_t),
        in_specs=[xspec,
                  pl.BlockSpec((None, 1, 1, 6 * D_MODEL), lambda b, t: (layer, b, 0, 0)),
                  _layer_spec(w_pw1, j), _layer_spec(b_pw1, j), _layer_spec(w_dw, j), _layer_spec(b_dw, j),
                  _layer_spec(cln_g, j), _layer_spec(cln_b, j), _layer_spec(w_pw2, j), _layer_spec(b_pw2, j),
                  _layer_spec(mix_g, layer), _layer_spec(mix_b, layer)],
        out_specs=[xspec, pl.BlockSpec((1, CONV_CTX, D_MODEL), lambda b, t: (b, 0, 0))],
        out_shape=[jax.ShapeDtypeStruct(x.shape, F32),
                   jax.ShapeDtypeStruct((n_b, CONV_CTX, D_MODEL), F32)],
        scratch_shapes=[pltpu.VMEM((tile_rows + CONV_HALO, D_MODEL), F32),
                        pltpu.VMEM((tile_rows, D_MODEL), F32)],
        compiler_params=_params("arbitrary", "arbitrary"),
        name=f"conv_p_l{layer}",
    )(x, mod, w_pw1, b_pw1, w_dw, b_dw, cln_g, cln_b, w_pw2, b_pw2, mix_g, mix_b)


def _conv_s_kernel(x_ref, mod_ref, st_ref, wpw1_ref, bpw1_ref, wdw_ref, bdw_ref, cg_ref, cb_ref,
                   wpw2_ref, bpw2_ref, g_ref, b_ref, o_ref, stn_ref):
    g, r, _ = x_ref.shape
    xs = [x_ref[:, t, :] for t in range(r)]
    mods = [mod_ref[:, 0, k * D_MODEL:(k + 1) * D_MODEL] for k in range(3)]
    h = jnp.concatenate([x * (1.0 + mods[1]) + mods[0] for x in xs], axis=0).astype(BF16)
    a = _glu_rows(h, wpw1_ref, bpw1_ref)
    full = [st_ref[i] for i in range(CONV_CTX)] + [a[t * g:(t + 1) * g] for t in range(r)]
    ys = []
    for t in range(r):
        y = jnp.broadcast_to(bdw_ref[...], (g, D_MODEL))
        for k in range(CONV_WIDTH):
            y = y + wdw_ref[k:k + 1, :] * full[t + k]
        ys.append(y)
    y = _layer_norm(jnp.concatenate(ys, axis=0), cg_ref[...], cb_ref[...])
    y = (y * jax.nn.sigmoid(y)).astype(BF16)
    out = _dot(y, wpw2_ref[...]) + bpw2_ref[...]
    for t in range(r):
        res = ALPHA * xs[t] + mods[2] * out[t * g:(t + 1) * g]
        o_ref[:, t, :] = _layer_norm(res, g_ref[...], b_ref[...])
    for i in range(CONV_CTX):
        stn_ref[i] = full[i + r]


def _conv_s_call(x, mod, layer, j, state_t, w_pw1, b_pw1, w_dw, b_dw, cln_g, cln_b, w_pw2, b_pw2, mix_g, mix_b,
                 tile_rows):
    s = _Stream(x, tile_rows)
    st_spec = pl.BlockSpec((None, CONV_CTX, s.g, D_MODEL), lambda i: (j, 0, i, 0))
    stn_spec = pl.BlockSpec((CONV_CTX, s.g, D_MODEL), lambda i: (0, i, 0))
    return pl.pallas_call(
        _conv_s_kernel,
        grid=(s.n_tiles,),
        in_specs=[s.x_spec(), s.mod_spec(layer), st_spec, _layer_spec(w_pw1, j), _layer_spec(b_pw1, j),
                  _layer_spec(w_dw, j), _layer_spec(b_dw, j), _layer_spec(cln_g, j), _layer_spec(cln_b, j),
                  _layer_spec(w_pw2, j), _layer_spec(b_pw2, j), _layer_spec(mix_g, layer),
                  _layer_spec(mix_b, layer)],
        out_specs=[s.x_spec(), stn_spec],
        out_shape=[s.x_shape(), jax.ShapeDtypeStruct((CONV_CTX, s.n_groups, D_MODEL), F32)],
        compiler_params=_params("arbitrary"),
        name=f"conv_s_l{layer}",
    )(x, mod, state_t, w_pw1, b_pw1, w_dw, b_dw, cln_g, cln_b, w_pw2, b_pw2, mix_g, mix_b)


def _rope_tile(x, cos, sin_signed):
    lane = lax.broadcasted_iota(jnp.int32, x.shape, 1)
    upper = pltpu.roll(x, LANES - MLA_ROPE // 2, 1)
    lower = pltpu.roll(x, MLA_ROPE // 2, 1)
    swapped = jnp.where((lane % MLA_ROPE) < MLA_ROPE // 2, upper, lower)
    return x * cos + swapped * sin_signed


MLA_QK = MLA_KV_RANK + MLA_ROPE


def _mla_proj_kernel(merged, x_ref, mod_ref, wdown_ref, gq_ref, gkv_ref, wuq_ref, wuk_ref, cos_ref, sin_ref,
                     *out_refs):
    parts = _row_parts(x_ref, 2)
    ds = [_dot(_modulated_rows(x_ref[gs, rs, :], mod_ref, 0, 1, gs), wdown_ref[...]) for gs, rs in parts]
    rows = ds[0].shape[0]
    for pi, d in enumerate(ds):
        _mla_proj_rows(merged, d, slice(pi * rows, (pi + 1) * rows), gq_ref, gkv_ref, wuq_ref, wuk_ref,
                       cos_ref, sin_ref, out_refs)


def _mla_proj_rows(merged, d, rsl, gq_ref, gkv_ref, wuq_ref, wuk_ref, cos_ref, sin_ref, out_refs):
    if merged:
        q_ref, ckv_ref, kr_ref, kcat_ref = out_refs
        qa_dst = lambda hd: q_ref.at[hd, rsl, :MLA_KV_RANK]
        qr_dst = lambda hd: q_ref.at[hd, rsl, MLA_KV_RANK:]
        q_dtype = q_ref.dtype
    else:
        qa_ref, qr_ref, ckv_ref, kr_ref = out_refs
        qa_dst = lambda hd: qa_ref.at[hd, rsl, :]
        qr_dst = lambda hd: qr_ref.at[hd, rsl, :]
        q_dtype = qa_ref.dtype
    q_lat = _rms_norm(d[:, :MLA_Q_RANK], gq_ref[...])
    ckv = _rms_norm(d[:, MLA_Q_RANK:MLA_Q_RANK + MLA_KV_RANK], gkv_ref[...])
    cos = cos_ref[rsl, :]
    sin = sin_ref[rsl, :]
    kr = _rope_tile(d[:, MLA_Q_RANK + MLA_KV_RANK:], cos, sin)[:, :MLA_ROPE]
    ckv_ref[rsl, :] = ckv
    kr_ref[rsl, :] = kr
    if merged:
        kcat_ref[rsl, :MLA_KV_RANK] = ckv.astype(BF16)
        kcat_ref[rsl, MLA_KV_RANK:] = kr.astype(BF16)
    q = _dot(q_lat.astype(BF16), wuq_ref[...])
    n_nope = MLA_HEADS * MLA_NOPE
    for hd in range(MLA_HEADS):
        qn = q[:, hd * MLA_NOPE:(hd + 1) * MLA_NOPE].astype(BF16)
        qa_dst(hd)[...] = (_dot(qn, wuk_ref[hd]) * MLA_SCALE).astype(q_dtype)
    for t in range(MLA_HEADS * MLA_ROPE // LANES):
        rot = _rope_tile(q[:, n_nope + t * LANES:n_nope + (t + 1) * LANES], cos, sin) * MLA_SCALE
        qr_dst(2 * t)[...] = rot[:, :MLA_ROPE].astype(q_dtype)
        qr_dst(2 * t + 1)[...] = rot[:, MLA_ROPE:].astype(q_dtype)


def _mla_proj_call(x, mod, layer, j, w_down, g_q, g_kv, w_uq, w_uk, cos, sin, tile_rows, merged, pos_tiles):
    s = _Stream(x, tile_rows)
    n = s.n_tokens
    if pos_tiles == 1:
        pos_spec = pl.BlockSpec((s.rows, LANES), lambda i: (0, 0))
    else:
        pos_spec = pl.BlockSpec((s.rows, LANES), lambda i: (i % pos_tiles, 0))
    if merged:
        out_specs = [pl.BlockSpec((MLA_HEADS, s.rows, MLA_QK), lambda i: (0, i, 0))]
        out_shape = [jax.ShapeDtypeStruct((MLA_HEADS, n, MLA_QK), BF16)]
    else:
        out_specs = [pl.BlockSpec((MLA_HEADS, s.rows, MLA_KV_RANK), lambda i: (0, i, 0)),
                     pl.BlockSpec((MLA_HEADS, s.rows, MLA_ROPE), lambda i: (0, i, 0))]
        out_shape = [jax.ShapeDtypeStruct((MLA_HEADS, n, MLA_KV_RANK), F32),
                     jax.ShapeDtypeStruct((MLA_HEADS, n, MLA_ROPE), F32)]
    out_specs += [s.rows_spec(MLA_KV_RANK), s.rows_spec(MLA_ROPE)]
    out_shape += [jax.ShapeDtypeStruct((n, MLA_KV_RANK), F32), jax.ShapeDtypeStruct((n, MLA_ROPE), F32)]
    if merged:
        out_specs.append(s.rows_spec(MLA_QK))
        out_shape.append(jax.ShapeDtypeStruct((n, MLA_QK), BF16))
    return pl.pallas_call(
        functools.partial(_mla_proj_kernel, merged),
        grid=(s.n_tiles,),
        in_specs=[s.x_spec(), s.mod_spec(layer), _layer_spec(w_down, j), _layer_spec(g_q, j),
                  _layer_spec(g_kv, j), _layer_spec(w_uq, j), _layer_spec(w_uk, j), pos_spec, pos_spec],
        out_specs=out_specs,
        out_shape=out_shape,
        compiler_params=_params("arbitrary"),
        name=f"mla_proj_l{layer}_g{s.g}",
    )(x, mod, w_down, g_q, g_kv, w_uq, w_uk, cos, sin)


ATTN_Q_TILE = 512
ATTN_K_TILE = 512
ATTN_ROW_PARTS = 8


def _attn_p_kernel(q_ref, k_ref, x_ref, mod_ref, wuv_ref, wo_ref, g_ref, b_ref, o_ref,
                   m_scr, l_scr, acc_scr, s_scr):
    i = pl.program_id(1)
    heads, tq, _ = q_ref.shape
    rows = heads * tq
    tk = ATTN_K_TILE
    m_scr[...] = jnp.full(m_scr.shape, NEG_BIG, F32)
    l_scr[...] = jnp.zeros(l_scr.shape, F32)
    acc_scr[...] = jnp.zeros(acc_scr.shape, F32)
    part_rows = rows // ATTN_ROW_PARTS
    parts = [slice(p * part_rows, (p + 1) * part_rows) for p in range(ATTN_ROW_PARTS)]

    def q_part(rs):
        return q_ref[rs.start // tq:rs.stop // tq].reshape(part_rows, MLA_QK)

    def keys(jb):
        return k_ref[pl.ds(pl.multiple_of(jb * tk, tk), tk), :]

    def absorb(rs, s, vb, masked):
        if masked:
            q_pos = lax.broadcasted_iota(jnp.int32, s.shape, 0) % tq
            k_pos = lax.broadcasted_iota(jnp.int32, s.shape, 1)
            s = jnp.where(k_pos <= q_pos, s, NEG_BIG)
        m_prev = m_scr[rs, :]
        m_next = jnp.maximum(m_prev, jnp.max(s, axis=1, keepdims=True))
        p = jnp.exp(s - jnp.concatenate([m_next] * (tk // LANES), axis=1))
        alpha = jnp.exp(m_prev - m_next)
        l_scr[rs, :] = alpha * l_scr[rs, :] + jnp.sum(p, axis=1, keepdims=True)
        m_scr[rs, :] = m_next
        acc_scr[rs, :] = (acc_scr[rs, :] * jnp.concatenate([alpha] * (MLA_KV_RANK // LANES), axis=1)
                          + _dot(p.astype(BF16), vb))

    kb0 = keys(0)
    for rs in parts:
        s_scr[rs, :] = _dot_nt(q_part(rs), kb0)

    def body(jb, carry):
        vb = keys(jb)[:, :MLA_KV_RANK]
        kb_next = keys(jb + 1)
        for rs in parts:
            s = s_scr[rs, :]
            s_scr[rs, :] = _dot_nt(q_part(rs), kb_next)
            absorb(rs, s, vb, False)
        return carry

    lax.fori_loop(0, i, body, 0)
    vb = keys(i)[:, :MLA_KV_RANK]
    for rs in parts:
        absorb(rs, s_scr[rs, :], vb, True)
    o_heads = []
    for hd in range(heads):
        rs = slice(hd * tq, (hd + 1) * tq)
        inv = 1.0 / l_scr[rs, :]
        ctx = (acc_scr[rs, :] * jnp.concatenate([inv] * (MLA_KV_RANK // LANES), axis=1)).astype(BF16)
        o_heads.append(_dot(ctx, wuv_ref[hd]).astype(BF16))
    y = _dot(jnp.concatenate(o_heads, axis=1), wo_ref[...])
    o_ref[...] = _residual_norm(x_ref[...], y, _mod(mod_ref, 2), g_ref, b_ref)


def _attn_p_call(q, kcat, x, mod, layer, j, w_uv, w_o, mix_g, mix_b):
    assert ATTN_Q_TILE == ATTN_K_TILE
    n_batch, seq, _ = x.shape
    nq = seq // ATTN_Q_TILE
    rows = MLA_HEADS * ATTN_Q_TILE
    xspec = pl.BlockSpec((1, ATTN_Q_TILE, D_MODEL), lambda b, i: (b, i, 0))
    return pl.pallas_call(
        _attn_p_kernel,
        grid=(n_batch, nq),
        in_specs=[pl.BlockSpec((MLA_HEADS, ATTN_Q_TILE, MLA_QK), lambda b, i: (0, b * nq + i, 0)),
                  pl.BlockSpec((seq, MLA_QK), lambda b, i: (b, 0)),
                  xspec,
                  pl.BlockSpec((None, 1, 1, 6 * D_MODEL), lambda b, i: (layer, b, 0, 0)),
                  _layer_spec(w_uv, j), _layer_spec(w_o, j), _layer_spec(mix_g, layer),
                  _layer_spec(mix_b, layer)],
        out_specs=xspec,
        out_shape=jax.ShapeDtypeStruct(x.shape, F32),
        scratch_shapes=[pltpu.VMEM((rows, LANES), F32), pltpu.VMEM((rows, LANES), F32),
                        pltpu.VMEM((rows, MLA_KV_RANK), F32), pltpu.VMEM((rows, ATTN_K_TILE), F32)],
        compiler_params=_params("arbitrary", "arbitrary"),
        name="mla_attn_p",
    )(q, kcat, x, mod, w_uv, w_o, mix_g, mix_b)


ATTN_S_SPLITS = 4


def _attn_s_kernel(j, pt_ref, qa_ref, qr_ref, kn_ref, rn_ref, cache_k, cache_rt, o_ref, kbuf, rbuf, sems):
    b = pl.program_id(0)
    nb = pl.num_programs(0)
    n_pages = pt_ref.shape[1]
    page = cache_k.shape[2]
    slot = b % 2

    def copies(pg, sl, p):
        keys = pl.ds(p * page, page)
        return (pltpu.make_async_copy(cache_k.at[j, pg], kbuf.at[sl, keys, :], sems.at[0, sl]),
                pltpu.make_async_copy(cache_rt.at[j, pg], rbuf.at[sl, :, keys], sems.at[1, sl]))

    def start_all(bi, sl):
        for p in range(n_pages):
            for c in copies(pt_ref[bi, p], sl, p):
                c.start(priority=p % 2)

    def wait_all(sl):
        for p in range(n_pages):
            for c in copies(0, sl, p):
                c.wait()

    @pl.when(b == 0)
    def _():
        start_all(0, 0)

    @pl.when(b + 1 < nb)
    def _():
        start_all(b + 1, 1 - slot)

    wait_all(slot)

    heads, t, _ = qa_ref.shape
    rows = heads * t
    qa = qa_ref[...].reshape(rows, MLA_KV_RANK).astype(BF16)
    qr = qr_ref[...].reshape(rows, MLA_ROPE).astype(BF16)
    kn = kn_ref[...].astype(BF16)
    rn = rn_ref[...].astype(BF16)

    s_new = _dot_nt(qa, kn) + _dot_nt(qr, rn)
    q_pos = lax.broadcasted_iota(jnp.int32, s_new.shape, 0) % t
    k_pos = lax.broadcasted_iota(jnp.int32, s_new.shape, 1)
    s_new = jnp.where(k_pos <= q_pos, s_new, NEG_BIG)
    m_new = jnp.max(s_new, axis=1, keepdims=True)
    p_new = jnp.exp(s_new - m_new)
    parts = [(m_new, jnp.sum(p_new, axis=1, keepdims=True), _dot(p_new.astype(BF16), kn))]

    span = n_pages * page // ATTN_S_SPLITS
    chunks = [slice(c * span, (c + 1) * span) for c in range(ATTN_S_SPLITS)]
    kps = [kbuf[slot, keys, :].astype(BF16) for keys in chunks]
    ss = [_dot_nt(qa, kp) + _dot(qr, rbuf[slot, :, keys].astype(BF16)) for kp, keys in zip(kps, chunks)]
    ms = [jnp.max(s, axis=1, keepdims=True) for s in ss]
    ps = [jnp.exp(s - m_c) for s, m_c in zip(ss, ms)]
    for m_c, p, kp in zip(ms, ps, kps):
        parts.append((m_c, jnp.sum(p, axis=1, keepdims=True), _dot(p.astype(BF16), kp)))

    m = functools.reduce(jnp.maximum, [pt[0] for pt in parts])
    scales = [jnp.exp(pt[0] - m) for pt in parts]
    denom = sum(sc * pt[1] for sc, pt in zip(scales, parts))
    ctx = sum(sc * pt[2] for sc, pt in zip(scales, parts)) / denom
    for hd in range(heads):
        o_ref[:, hd * MLA_KV_RANK:(hd + 1) * MLA_KV_RANK] = ctx[hd * t:(hd + 1) * t]


def _attn_s_call(page_table, qa, qr, ckv_new, kr_new, cache_k, cache_r, j, n_batch, t):
    past = page_table.shape[1] * cache_k.shape[2]
    grid_spec = pltpu.PrefetchScalarGridSpec(
        num_scalar_prefetch=1,
        grid=(n_batch,),
        in_specs=[pl.BlockSpec((MLA_HEADS, t, MLA_KV_RANK), lambda b, pt: (0, b, 0)),
                  pl.BlockSpec((MLA_HEADS, t, MLA_ROPE), lambda b, pt: (0, b, 0)),
                  pl.BlockSpec((t, MLA_KV_RANK), lambda b, pt: (b, 0)),
                  pl.BlockSpec((t, MLA_ROPE), lambda b, pt: (b, 0)),
                  pl.BlockSpec(memory_space=pl.ANY),
                  pl.BlockSpec(memory_space=pl.ANY)],
        out_specs=pl.BlockSpec((t, MLA_HEADS * MLA_KV_RANK), lambda b, pt: (b, 0)),
        scratch_shapes=[pltpu.VMEM((2, past, MLA_KV_RANK), F32),
                        pltpu.VMEM((2, MLA_ROPE, past), F32),
                        pltpu.SemaphoreType.DMA((2, 2))],
    )
    return pl.pallas_call(
        functools.partial(_attn_s_kernel, j),
        grid_spec=grid_spec,
        out_shape=jax.ShapeDtypeStruct((n_batch * t, MLA_HEADS * MLA_KV_RANK), F32),
        compiler_params=_params("arbitrary"),
        name="mla_attn_s",
    )(page_table, qa, qr, ckv_new, kr_new, cache_k, cache_r)


def _mla_out_kernel(x_ref, mod_ref, ctx_ref, wuv_ref, wo_ref, g_ref, b_ref, o_ref):
    x = x_ref[...]
    parts = []
    for hd in range(MLA_HEADS):
        c = ctx_ref[:, hd * MLA_KV_RANK:(hd + 1) * MLA_KV_RANK].astype(BF16)
        parts.append(_dot(c, wuv_ref[hd]).astype(BF16))
    o = jnp.concatenate(parts, axis=1)
    y = _dot(o, wo_ref[...])
    o_ref[...] = _residual_norm(x, y, _mod(mod_ref, 2), g_ref, b_ref)


def _mla_out_call(x, mod, layer, j, ctx, w_uv, w_o, mix_g, mix_b, tile_rows):
    s = _Stream(x, tile_rows)
    return pl.pallas_call(
        _mla_out_kernel,
        grid=(s.n_tiles,),
        in_specs=[s.x_spec(), s.mod_spec(layer), s.rows_spec(MLA_HEADS * MLA_KV_RANK),
                  _layer_spec(w_uv, j), _layer_spec(w_o, j), _layer_spec(mix_g, layer),
                  _layer_spec(mix_b, layer)],
        out_specs=s.x_spec(),
        out_shape=s.x_shape(),
        compiler_params=_params("arbitrary"),
        name=f"mla_out_l{layer}_g{s.g}",
    )(x, mod, ctx, w_uv, w_o, mix_g, mix_b)


def _row3(a):
    return a.reshape(a.shape[0], 1, a.shape[1])


def _rope_tables(pos):
    half = MLA_ROPE // 2
    inv = ROPE_THETA ** (-jnp.arange(half, dtype=F32) / half)
    ang = pos.astype(F32)[:, None] * inv[None, :]
    cos, sin = jnp.cos(ang), jnp.sin(ang)
    cos64 = jnp.concatenate([cos, cos], -1)
    sin64 = jnp.concatenate([-sin, sin], -1)
    return jnp.tile(cos64, (1, LANES // MLA_ROPE)), jnp.tile(sin64, (1, LANES // MLA_ROPE))


def kernel(x_prompt, x_sample, cache_ckv, cache_krope, state_conv, page_table, c_prompt, c_sample,
           ada_w, ada_b, ln_mix_g, ln_mix_b, ln_ffn_g, ln_ffn_b, ffn_w_gate, ffn_w_up, ffn_w_down,
           gmlp_w_in, gmlp_ln_g, gmlp_ln_b, gmlp_w_s, gmlp_b_s, gmlp_w_out,
           conv_w_pw1, conv_b_pw1, conv_w_dw, conv_b_dw, conv_ln_g, conv_ln_b, conv_w_pw2, conv_b_pw2,
           mla_w_down, mla_g_q, mla_g_kv, mla_w_uq, mla_w_uk, mla_w_uv, mla_w_o):
    n_b, seq, _ = x_prompt.shape
    n_db, dec_seq, _ = x_sample.shape
    past_len = page_table.shape[1] * cache_ckv.shape[2]

    c_p = jnp.pad(c_prompt, ((0, SUBLANES - n_b), (0, 0)))
    mod_p, mod_s = _ada_call(c_p, c_sample, ada_w, ada_b)

    ln_mix_g, ln_mix_b, ln_ffn_g, ln_ffn_b = map(_row3, (ln_mix_g, ln_mix_b, ln_ffn_g, ln_ffn_b))
    wg, wu, wd = (w.astype(BF16) for w in (ffn_w_gate, ffn_w_up, ffn_w_down))

    g_w_in, g_w_out = gmlp_w_in.astype(BF16), gmlp_w_out.astype(BF16)
    g_ln_g, g_ln_b = _row3(gmlp_ln_g), _row3(gmlp_ln_b)
    reps = GMLP_CHUNK // dec_seq
    eye = jnp.eye(reps, dtype=F32)

    c_w_pw1, c_w_pw2 = conv_w_pw1.astype(BF16), conv_w_pw2.astype(BF16)
    c_b_pw1, c_b_dw, c_ln_g, c_ln_b, c_b_pw2 = map(_row3, (conv_b_pw1, conv_b_dw, conv_ln_g, conv_ln_b,
                                                            conv_b_pw2))
    c_w_dw8 = jnp.broadcast_to(conv_w_dw[:, :, None, :], conv_w_dw.shape[:2] + (SUBLANES, D_MODEL))
    state_t = jnp.swapaxes(state_conv, 1, 2)

    n_mla = mla_w_down.shape[0]
    m_w_down = jnp.pad(mla_w_down, ((0, 0), (0, 0), (0, MLA_DOWN_PAD - mla_w_down.shape[-1]))).astype(BF16)
    w_uq = mla_w_uq.reshape(n_mla, MLA_Q_RANK, MLA_HEADS, MLA_NOPE + MLA_ROPE)
    m_w_uq = jnp.concatenate([w_uq[..., :MLA_NOPE].reshape(n_mla, MLA_Q_RANK, -1),
                              w_uq[..., MLA_NOPE:].reshape(n_mla, MLA_Q_RANK, -1)], -1).astype(BF16)
    m_w_uk = mla_w_uk.transpose(0, 2, 3, 1).astype(BF16)
    m_w_uv = mla_w_uv.transpose(0, 2, 1, 3).astype(BF16)
    m_w_o = mla_w_o.astype(BF16)
    m_g_q, m_g_kv = _row3(mla_g_q), _row3(mla_g_kv)
    cos_p, sin_p = _rope_tables(jnp.arange(seq))
    cos_s, sin_s = _rope_tables(past_len + jnp.arange(dec_seq))

    ffn_tile = 512
    gm_tile = 512
    conv_tile = 512
    mla_tile = 512
    s_reps = mla_tile // dec_seq
    cos_s, sin_s = jnp.tile(cos_s, (s_reps, 1)), jnp.tile(sin_s, (s_reps, 1))

    xp, xs = x_prompt, x_sample
    ckv_p_rows, kr_p_rows, ckv_s_rows, kr_s_rows = [], [], [], []
    conv_p_states, conv_s_states, gmlp_v_rows = [], [], []
    for i in range(DEPTH):
        kind, j = i % N_MIXERS, i // N_MIXERS
        if kind == 0:
            wmix_p = gmlp_w_s[j].astype(BF16)
            bmix_p = gmlp_b_s[j].T
            corner = gmlp_w_s[j, :, :dec_seq, :dec_seq]
            wmix_s = jnp.einsum('ab,gts->gatbs', eye, corner).reshape(
                GMLP_GROUPS, GMLP_CHUNK, GMLP_CHUNK).astype(BF16)
            bmix_s = jnp.tile(gmlp_b_s[j, :, :dec_seq].T, (reps, 1))
            xp, xs, v_s = _gmlp_call(xp, mod_p, xs, mod_s, i, j, g_w_in, g_ln_g, g_ln_b, wmix_p, bmix_p,
                                     wmix_s, bmix_s, g_w_out, ln_mix_g, ln_mix_b, gm_tile)
            gmlp_v_rows.append(v_s)
        elif kind == 1:
            xp, st_p = _conv_p_call(xp, mod_p, i, j, c_w_pw1, c_b_pw1, c_w_dw8, c_b_dw, c_ln_g, c_ln_b,
                                    c_w_pw2, c_b_pw2, ln_mix_g, ln_mix_b, conv_tile)
            xs, st_s = _conv_s_call(xs, mod_s, i, j, state_t, c_w_pw1, c_b_pw1, conv_w_dw, c_b_dw,
                                    c_ln_g, c_ln_b, c_w_pw2, c_b_pw2, ln_mix_g, ln_mix_b, conv_tile)
            conv_p_states.append(st_p)
            conv_s_states.append(jnp.swapaxes(st_s, 0, 1))
        else:
            q_p, ckv, kr, kcat = _mla_proj_call(
                xp, mod_p, i, j, m_w_down, m_g_q, m_g_kv, m_w_uq, m_w_uk, cos_p, sin_p, mla_tile, True,
                seq // mla_tile)
            xp = _attn_p_call(q_p, kcat, xp, mod_p, i, j, m_w_uv, m_w_o, ln_mix_g, ln_mix_b)
            ckv_p_rows.append(ckv.reshape(n_b, seq, MLA_KV_RANK))
            kr_p_rows.append(kr.reshape(n_b, seq, MLA_ROPE))
            qa, qr, ckv, kr = _mla_proj_call(
                xs, mod_s, i, j, m_w_down, m_g_q, m_g_kv, m_w_uq, m_w_uk, cos_s, sin_s, mla_tile, False, 1)
            ctx_s = _attn_s_call(page_table, qa, qr, ckv, kr, cache_ckv, jnp.swapaxes(cache_krope, 2, 3), j,
                                 n_db, dec_seq)
            xs = _mla_out_call(xs, mod_s, i, j, ctx_s, m_w_uv, m_w_o, ln_mix_g, ln_mix_b, mla_tile)
            ckv_s_rows.append(ckv.reshape(n_db, dec_seq, MLA_KV_RANK))
            kr_s_rows.append(kr.reshape(n_db, dec_seq, MLA_ROPE))
        xp, xs = _ffn_call(xp, mod_p, xs, mod_s, i, wg, wu, wd, ln_ffn_g, ln_ffn_b, ffn_tile)
    return (xp, xs, jnp.stack(ckv_p_rows), jnp.stack(kr_p_rows), jnp.stack(ckv_s_rows), jnp.stack(kr_s_rows),
            jnp.stack(conv_p_states), jnp.stack(conv_s_states), jnp.stack(gmlp_v_rows))
```
